```python
import jax, jax.numpy as jnp
from jax import lax
import numpy as np

D_MODEL = 1024
BATCH = 8
SEQ = 4096
DEPTH = 2

CHUNK = 64
Q_BLOCK = 128
P_DIM = 256
MLA_HEADS = 8
MLA_NOPE = 64
MLA_ROPE = 32
MLA_V = 64
Q_LORA = 256
KV_LORA = 256
ROPE_BASE = 10000.0
SB_HEADS = 8
SB_DIM = 64
FOX_HEADS = 16
FOX_DIM = 64
D_FF = ((-(-8 * D_MODEL // 3)) + 255) // 256 * 256
DEEPNORM_ALPHA = (2.0 * DEPTH) ** 0.25
DEEPNORM_BETA = (8.0 * DEPTH) ** -0.25
N_EVEN = (DEPTH + 1) // 2
N_ODD = DEPTH // 2
IN_A = Q_LORA + KV_LORA + MLA_ROPE
IN_B = 3 * SB_HEADS * SB_DIM
IN_C = 3 * FOX_HEADS * FOX_DIM + FOX_HEADS
MIX_A = MLA_HEADS * MLA_V + SB_HEADS * SB_DIM
MIX_C = FOX_HEADS * FOX_DIM

kernel_name = "hybrid_mla_stickbreak_fox_deepnorm"


def layer_norm(x, g, b, eps=1e-5):
    xf = x.astype(jnp.float32)
    mu = jnp.mean(xf, -1, keepdims=True)
    var = jnp.mean(jnp.square(xf - mu), -1, keepdims=True)
    return ((xf - mu) * lax.rsqrt(var + eps) * g + b).astype(x.dtype)


def rms_norm(x, g, eps=1e-6):
    xf = x.astype(jnp.float32)
    return (xf * lax.rsqrt(jnp.mean(jnp.square(xf), -1, keepdims=True) + eps) * g).astype(x.dtype)


def rope_tables(seq, dim):
    inv = 1.0 / (ROPE_BASE ** (jnp.arange(0, dim, 2, dtype=jnp.float32) / dim))
    ang = jnp.arange(seq, dtype=jnp.float32)[:, None] * inv[None, :]
    return jnp.cos(ang), jnp.sin(ang)


def apply_rope(x, cos, sin):
    x1, x2 = jnp.split(x, 2, axis=-1)
    return jnp.concatenate([x1 * cos - x2 * sin, x1 * sin + x2 * cos], -1).astype(x.dtype)


def sweep_query_blocks(block_fn, q):
    b, h, s, d = q.shape
    nb = s // Q_BLOCK
    q_blocks = q.reshape(b, h, nb, Q_BLOCK, d).transpose(2, 0, 1, 3, 4)
    out = lax.map(lambda a: block_fn(a[0], a[1]), (q_blocks, jnp.arange(nb)))
    dv = out.shape[-1]
    return out.transpose(1, 0, 3, 2, 4).reshape(b, s, h * dv)


def mla_block(q_blk, blk, k, v):
    t = blk * Q_BLOCK + jnp.arange(Q_BLOCK)
    s = jnp.arange(k.shape[2])
    allowed = (s[None, :] // CHUNK) <= (t[:, None] // CHUNK)
    logits = jnp.einsum('bhqd,bhkd->bhqk', q_blk, k).astype(jnp.float32) * (MLA_NOPE + MLA_ROPE) ** -0.5
    w = jax.nn.softmax(jnp.where(allowed, logits, -jnp.inf), axis=-1)
    return jnp.einsum('bhqk,bhkd->bhqd', w.astype(v.dtype), v)


def stick_breaking_block(q_blk, blk, k, v):
    t = blk * Q_BLOCK + jnp.arange(Q_BLOCK)
    s = jnp.arange(k.shape[2])
    past = s[None, :] < t[:, None]
    z = jnp.einsum('bhqd,bhkd->bhqk', q_blk, k).astype(jnp.float32) * SB_DIM ** -0.5
    log_beta = jax.nn.log_sigmoid(z)
    log_rem = jnp.where(past, jax.nn.log_sigmoid(-z), 0.0)
    between = lax.cumsum(log_rem, axis=3, reverse=True) - log_rem
    att = jnp.where(past, jnp.exp(log_beta + between), 0.0)
    return jnp.einsum('bhqk,bhkd->bhqd', att.astype(v.dtype), v)


def forgetting_block(q_blk, blk, k, v, dcum):
    t = blk * Q_BLOCK + jnp.arange(Q_BLOCK)
    s = jnp.arange(k.shape[2])
    causal = s[None, :] <= t[:, None]
    d_t = lax.dynamic_slice_in_dim(dcum, blk * Q_BLOCK, Q_BLOCK, axis=2)
    logits = (jnp.einsum('bhqd,bhkd->bhqk', q_blk, k).astype(jnp.float32) * FOX_DIM ** -0.5
              + d_t[..., :, None] - dcum[..., None, :])
    w = jax.nn.softmax(jnp.where(causal, logits, -jnp.inf), axis=-1)
    return jnp.einsum('bhqk,bhkd->bhqd', w.astype(v.dtype), v)


def mixer_mla_sb(x, w_in, q_norm_g, w_uq, kv_norm_g, w_ukv, w_out, cos, sin):
    b, s, _ = x.shape
    h = x @ w_in
    c_q, c_kv, k_rope, sb_qkv = jnp.split(h, [Q_LORA, Q_LORA + KV_LORA, IN_A], axis=-1)
    q = (rms_norm(c_q, q_norm_g) @ w_uq).reshape(b, s, MLA_HEADS, MLA_NOPE + MLA_ROPE).transpose(0, 2, 1, 3)
    q_nope, q_rope = jnp.split(q, [MLA_NOPE], axis=-1)
    q = jnp.concatenate([q_nope, apply_rope(q_rope, cos, sin)], -1)
    kv = (rms_norm(c_kv, kv_norm_g) @ w_ukv).reshape(b, s, MLA_HEADS, MLA_NOPE + MLA_V).transpose(0, 2, 1, 3)
    k_nope, v_a = jnp.split(kv, [MLA_NOPE], axis=-1)
    k_rope = apply_rope(k_rope[:, None], cos, sin)
    k_a = jnp.concatenate([k_nope, jnp.broadcast_to(k_rope, (b, MLA_HEADS, s, MLA_ROPE))], -1)
    o_a = sweep_query_blocks(lambda qb, i: mla_block(qb, i, k_a, v_a), q)
    q_b, k_b, v_b = sb_qkv.reshape(b, s, 3, SB_HEADS, SB_DIM).transpose(2, 0, 3, 1, 4)
    o_b = sweep_query_blocks(lambda qb, i: stick_breaking_block(qb, i, k_b, v_b), q_b)
    return jnp.concatenate([o_a, o_b], -1) @ w_out


def mixer_fox(x, w_in, b_f, w_out):
    b, s, _ = x.shape
    h = x @ w_in
    qkv, f_logit = jnp.split(h, [3 * FOX_HEADS * FOX_DIM], axis=-1)
    q, k, v = qkv.reshape(b, s, 3, FOX_HEADS, FOX_DIM).transpose(2, 0, 3, 1, 4)
    log_f = jax.nn.log_sigmoid((f_logit + b_f).astype(jnp.float32))
    dcum = lax.cumsum(log_f, axis=1).transpose(0, 2, 1)
    o = sweep_query_blocks(lambda qb, i: forgetting_block(qb, i, k, v, dcum), q)
    return o @ w_out


def swiglu(x, w1, w3, w2):
    return (jax.nn.silu(x @ w1) * (x @ w3)) @ w2


def setup_inputs(seed: int = 0) -> dict:
    key = jax.random.key(seed)
    ks = iter(jax.random.split(key, 32))

    def dense(shape, fan_in, scale=1.0):
        return jax.random.normal(next(ks), shape, jnp.float32) * (fan_in ** -0.5) * scale

    def gain(shape):
        return 1.0 + 0.02 * jax.random.normal(next(ks), shape, jnp.float32)

    def bias(shape):
        return 0.02 * jax.random.normal(next(ks), shape, jnp.float32)

    return {
        "x": jax.random.normal(next(ks), (BATCH, SEQ, D_MODEL), jnp.float32),
        "p": jax.random.normal(next(ks), (DEPTH, BATCH, SEQ, P_DIM), jnp.float32),
        "a_w_in": dense((N_EVEN, D_MODEL, IN_A + IN_B), D_MODEL),
        "a_q_norm": gain((N_EVEN, Q_LORA)),
        "a_w_uq": dense((N_EVEN, Q_LORA, MLA_HEADS * (MLA_NOPE + MLA_ROPE)), Q_LORA),
        "a_kv_norm": gain((N_EVEN, KV_LORA)),
        "a_w_ukv": dense((N_EVEN, KV_LORA, MLA_HEADS * (MLA_NOPE + MLA_V)), KV_LORA),
        "a_w_out": dense((N_EVEN, MIX_A, D_MODEL), MIX_A, DEEPNORM_BETA),
        "c_w_in": dense((N_ODD, D_MODEL, IN_C), D_MODEL),
        "c_b_f": jax.random.uniform(next(ks), (N_ODD, FOX_HEADS), jnp.float32, 1.0, 4.0),
        "c_w_out": dense((N_ODD, MIX_C, D_MODEL), MIX_C, DEEPNORM_BETA),
        "ffn_w1": dense((DEPTH, D_MODEL, D_FF), D_MODEL),
        "ffn_w3": dense((DEPTH, D_MODEL, D_FF), D_MODEL),
        "ffn_w2": dense((DEPTH, D_FF, D_MODEL), D_FF, DEEPNORM_BETA),
        "ln1_g": gain((DEPTH, D_MODEL)),
        "ln1_b": bias((DEPTH, D_MODEL)),
        "ln2_g": gain((DEPTH, D_MODEL)),
        "ln2_b": bias((DEPTH, D_MODEL)),
        "ple_w_proj": dense((DEPTH, P_DIM, D_MODEL), P_DIM),
        "ple_w_gate": dense((DEPTH, D_MODEL, D_MODEL), D_MODEL),
        "ple_b_gate": bias((DEPTH, D_MODEL)),
    }


def reference(x, p, a_w_in, a_q_norm, a_w_uq, a_kv_norm, a_w_ukv, a_w_out,
              c_w_in, c_b_f, c_w_out, ffn_w1, ffn_w3, ffn_w2,
              ln1_g, ln1_b, ln2_g, ln2_b, ple_w_proj, ple_w_gate, ple_b_gate):
    cos, sin = rope_tables(x.shape[1], MLA_ROPE)
    for i in range(DEPTH):
        j = i // 2
        if i % 2 == 0:
            mix = mixer_mla_sb(x, a_w_in[j], a_q_norm[j], a_w_uq[j], a_kv_norm[j],
                               a_w_ukv[j], a_w_out[j], cos, sin)
        else:
            mix = mixer_fox(x, c_w_in[j], c_b_f[j], c_w_out[j])
        x = layer_norm(DEEPNORM_ALPHA * x + mix, ln1_g[i], ln1_b[i])
        x = layer_norm(DEEPNORM_ALPHA * x + swiglu(x, ffn_w1[i], ffn_w3[i], ffn_w2[i]), ln2_g[i], ln2_b[i])
        x = x + jax.nn.sigmoid(x @ ple_w_gate[i] + ple_b_gate[i]) * (p[i] @ ple_w_proj[i])
    return x
```

```python
import functools

import jax
import jax.numpy as jnp
import numpy as np
from jax import lax
from jax.experimental import pallas as pl
from jax.experimental.pallas import tpu as pltpu

D_MODEL = 1024
SEQ_CHUNK = 64
P_DIM = 256
MLA_HEADS = 8
MLA_NOPE = 64
MLA_ROPE = 32
Q_LORA = 256
KV_LORA = 256
ROPE_BASE = 10000.0
SB_HEADS = 8
FOX_HEADS = 16
HEAD_DIM = 64
DEPTH = 2
DEEPNORM_ALPHA = (2.0 * DEPTH) ** 0.25
MLA_SCALE = (MLA_NOPE + MLA_ROPE) ** -0.5
HEAD_SCALE = HEAD_DIM ** -0.5

LANES = 128
BF16_ROWS = 16
VMEM_LIMIT = 56 * 1024 * 1024

TOK_TILE = 512
TQ = 256
TK_SOFTMAX = 256
TK_SB = 128
NEG = -1e30

bf16 = jnp.bfloat16
f32 = jnp.float32


def _dot(a, b):
    return jnp.dot(a, b, preferred_element_type=f32)


def _dot_nt(a, b):
    return lax.dot_general(a, b, (((1,), (1,)), ((), ())), preferred_element_type=f32)


def _layer_norm(z, g, b):
    mu = jnp.mean(z, axis=-1, keepdims=True)
    zc = z - mu
    var = jnp.mean(zc * zc, axis=-1, keepdims=True)
    return zc * lax.rsqrt(var + 1e-5) * g + b


def _rms_norm(c, g):
    return c * lax.rsqrt(jnp.mean(c * c, axis=-1, keepdims=True) + 1e-6) * g


def _log_sigmoid(a):
    return jnp.minimum(a, 0.0) - jnp.log(1.0 + jnp.exp(-jnp.abs(a)))


def _const_spec(shape):
    nd = len(shape)
    return pl.BlockSpec(shape, lambda *_: (0,) * nd, pipeline_mode=pl.Buffered(1))


def _params(*sem):
    return pltpu.CompilerParams(dimension_semantics=sem, vmem_limit_bytes=VMEM_LIMIT)


def _proj_a_kernel(x_ref, wlat_ref, wkr_ref, wsbq_ref, wsbk_ref, wsbv_ref, gq_ref, gkv_ref,
                   wqn_ref, wqra_ref, wqrb_ref, wkn_ref, wv_ref, cos_ref, sin_ref, cost_ref, sint_ref,
                   qnt_ref, qrt_ref, kn_ref, kr_ref, vat_ref, qbt_ref, kb_ref, vbt_ref):
    xb = x_ref[0].astype(bf16)
    lat = _dot(xb, wlat_ref[...])
    cq = _rms_norm(lat[:, :Q_LORA], gq_ref[...]).astype(bf16)
    ckv = _rms_norm(lat[:, Q_LORA:], gkv_ref[...]).astype(bf16)

    qnt_ref[0] = (_dot_nt(wqn_ref[...], cq) * MLA_SCALE).astype(bf16)
    qa = _dot_nt(wqra_ref[...], cq)
    qb = _dot_nt(wqrb_ref[...], cq)
    cost = cost_ref[...]
    sint = sint_ref[...]
    for h in range(MLA_HEADS):
        sl = slice(h * LANES, (h + 1) * LANES)
        qrt_ref[0, sl, :] = ((qa[sl] * cost + qb[sl] * sint) * MLA_SCALE).astype(bf16)

    kn_ref[0] = _dot(ckv, wkn_ref[...]).astype(bf16)
    kr2 = _dot(xb, wkr_ref[...])
    kr_ref[0] = (kr2[:, :LANES] * cos_ref[...] + kr2[:, LANES:] * sin_ref[...]).astype(bf16)

    vat = _dot_nt(wv_ref[...], ckv).astype(bf16)
    tm = vat.shape[1]
    for c in range(tm // TK_SOFTMAX):
        vat_ref[0, c] = vat[:, c * TK_SOFTMAX:(c + 1) * TK_SOFTMAX]

    qbt_ref[0] = _dot_nt(wsbq_ref[...], xb).astype(bf16)
    kb_ref[0] = _dot(xb, wsbk_ref[...]).astype(bf16)
    vbt = _dot_nt(wsbv_ref[...], xb).astype(bf16)
    for c in range(tm // TK_SB):
        vbt_ref[0, c] = vbt[:, c * TK_SB:(c + 1) * TK_SB]


def _proj_a(x, w, cos, sin, cost, sint):
    B, S, D = x.shape
    tm = TOK_TILE
    na = MLA_HEADS * HEAD_DIM
    nb = SB_HEADS * HEAD_DIM
    consts = [w["lat"], w["kr"], w["sbq_t"], w["sbk"], w["sbv_t"], w["gq"], w["gkv"],
              w["qn_t"], w["qra_t"], w["qrb_t"], w["kn"], w["v_t"]]
    in_specs = [pl.BlockSpec((1, tm, D), lambda b, i: (b, i, 0))]
    in_specs += [_const_spec(c.shape) for c in consts]
    in_specs += [pl.BlockSpec((tm, LANES), lambda b, i: (i, 0)),
                 pl.BlockSpec((tm, LANES), lambda b, i: (i, 0)),
                 pl.BlockSpec((LANES, tm), lambda b, i: (0, i)),
                 pl.BlockSpec((LANES, tm), lambda b, i: (0, i))]
    out_shape = [
        jax.ShapeDtypeStruct((B, na, S), bf16),
        jax.ShapeDtypeStruct((B, MLA_HEADS * LANES, S), bf16),
        jax.ShapeDtypeStruct((B, S, na), bf16),
        jax.ShapeDtypeStruct((B, S, LANES), bf16),
        jax.ShapeDtypeStruct((B, S // TK_SOFTMAX, na, TK_SOFTMAX), bf16),
        jax.ShapeDtypeStruct((B, nb, S), bf16),
        jax.ShapeDtypeStruct((B, S, nb), bf16),
        jax.ShapeDtypeStruct((B, S // TK_SB, nb, TK_SB), bf16),
    ]
    out_specs = [
        pl.BlockSpec((1, na, tm), lambda b, i: (b, 0, i)),
        pl.BlockSpec((1, MLA_HEADS * LANES, tm), lambda b, i: (b, 0, i)),
        pl.BlockSpec((1, tm, na), lambda b, i: (b, i, 0)),
        pl.BlockSpec((1, tm, LANES), lambda b, i: (b, i, 0)),
        pl.BlockSpec((1, tm // TK_SOFTMAX, na, TK_SOFTMAX), lambda b, i: (b, i, 0, 0)),
        pl.BlockSpec((1, nb, tm), lambda b, i: (b, 0, i)),
        pl.BlockSpec((1, tm, nb), lambda b, i: (b, i, 0)),
        pl.BlockSpec((1, tm // TK_SB, nb, TK_SB), lambda b, i: (b, i, 0, 0)),
    ]
    return pl.pallas_call(
        _proj_a_kernel, grid=(B, S // tm), in_specs=in_specs, out_specs=out_specs, out_shape=out_shape,
        compiler_params=_params("parallel", "parallel"), name="proj_a",
    )(x, *consts, cos, sin, cost, sint)


def _pair_queries(q1):
    row = lax.broadcasted_iota(jnp.int32, q1.shape, 0)
    zero = jnp.zeros_like(q1)
    return jnp.concatenate([jnp.where(row < HEAD_DIM, q1, zero), jnp.where(row >= HEAD_DIM, q1, zero)], axis=1)


def _positions(shape, key_start, query_start, tq):
    kpos = key_start + lax.broadcasted_iota(jnp.int32, shape, 0)
    col = lax.broadcasted_iota(jnp.int32, shape, 1)
    qpos = query_start + jnp.where(col >= tq, col - tq, col)
    return kpos, qpos


def _unpair_output(acc, inv, tq):
    if inv is not None:
        acc = acc * inv
    ot = jnp.concatenate([acc[:HEAD_DIM, :tq], acc[HEAD_DIM:2 * HEAD_DIM, tq:]], axis=0)
    return ot.T


def _attn_softmax_kernel(q1_ref, qx_ref, k1_ref, kx_ref, vt_ref, o_ref, m_ref, acc_ref, *, tq, tk, chunk_shift):
    i = pl.program_id(2)
    qx = qx_ref[0]
    q2t = jnp.concatenate([_pair_queries(q1_ref[0]),
                           jnp.concatenate([qx[:LANES], qx[LANES:]], axis=1)], axis=0)
    m_ref[...] = jnp.full(m_ref.shape, NEG, f32)
    acc_ref[...] = jnp.zeros(acc_ref.shape, f32)
    ones = jnp.ones((BF16_ROWS, tk), bf16)

    def step(j, masked):
        ks = pl.multiple_of(j * tk, tk)
        kb = jnp.concatenate([k1_ref[0, pl.ds(ks, tk), :], kx_ref[0, pl.ds(ks, tk), :]], axis=1)
        s = _dot(kb, q2t)
        if masked:
            kpos, qpos = _positions(s.shape, ks, i * tq, tq)
            ok = (kpos >> chunk_shift) <= (qpos >> chunk_shift)
            s = jnp.where(ok, s, NEG)
        m_old = m_ref[...]
        m_new = jnp.maximum(m_old, jnp.max(s, axis=0, keepdims=True))
        alpha = jnp.exp(m_old - m_new)
        p = jnp.exp(s - m_new).astype(bf16)
        vta = jnp.concatenate([vt_ref[0, j], ones], axis=0)
        acc_ref[...] = alpha * acc_ref[...] + _dot(vta, p)
        m_ref[...] = m_new

    def body(j, carry):
        step(j, False)
        return carry

    lax.fori_loop(0, i, body, 0)
    step(i, True)
    acc = acc_ref[...]
    inv = 1.0 / acc[LANES:LANES + 1, :]
    o_ref[0] = _unpair_output(acc, inv, tq).astype(o_ref.dtype)


def _attn_softmax(q1t, qxt, k1, kx, vt, *, kx_shared, chunk_shift, name):
    B, n1, S = q1t.shape
    pairs = n1 // LANES
    tq, tk = TQ, TK_SOFTMAX
    assert tq == tk
    kx_map = (lambda b, p, i: (b, 0, 0)) if kx_shared else (lambda b, p, i: (b, 0, p))
    kern = functools.partial(_attn_softmax_kernel, tq=tq, tk=tk, chunk_shift=chunk_shift)
    return pl.pallas_call(
        kern, grid=(B, pairs, S // tq),
        in_specs=[pl.BlockSpec((1, LANES, tq), lambda b, p, i: (b, p, i)),
                  pl.BlockSpec((1, 2 * LANES, tq), lambda b, p, i: (b, p, i)),
                  pl.BlockSpec((1, S, LANES), lambda b, p, i: (b, 0, p)),
                  pl.BlockSpec((1, S, LANES), kx_map),
                  pl.BlockSpec((1, S // tk, LANES, tk), lambda b, p, i: (b, 0, p, 0))],
        out_specs=pl.BlockSpec((1, tq, LANES), lambda b, p, i: (b, i, p)),
        out_shape=jax.ShapeDtypeStruct((B, S, n1), bf16),
        scratch_shapes=[pltpu.VMEM((1, 2 * tq), f32), pltpu.VMEM((LANES + BF16_ROWS, 2 * tq), f32)],
        compiler_params=_params("parallel", "parallel", "arbitrary"), name=name,
    )(q1t, qxt, k1, kx, vt)


def _attn_sb_kernel(qt_ref, k_ref, vt_ref, o_ref, r_ref, acc_ref, *, tq, tk):
    i = pl.program_id(2)
    q2t = _pair_queries(qt_ref[0])
    r_ref[...] = jnp.zeros(r_ref.shape, f32)
    acc_ref[...] = jnp.zeros(acc_ref.shape, f32)
    tri = (lax.broadcasted_iota(jnp.int32, (tk, tk), 0) < lax.broadcasted_iota(jnp.int32, (tk, tk), 1)).astype(bf16)

    def step(j, masked):
        ks = pl.multiple_of(j * tk, tk)
        z = _dot(k_ref[0, pl.ds(ks, tk), :], q2t)
        log_beta = _log_sigmoid(z)
        log_rem = log_beta - z
        if masked:
            kpos, qpos = _positions(z.shape, ks, i * tq, tq)
            past = kpos < qpos
            log_rem = jnp.where(past, log_rem, 0.0)
        hi = log_rem.astype(bf16)
        lo = (log_rem - hi.astype(f32)).astype(bf16)
        between = _dot(tri, hi) + _dot(tri, lo) + r_ref[...]
        r_ref[...] = between[0:1, :] + log_rem[0:1, :]
        att = jnp.exp(log_beta + between)
        if masked:
            att = jnp.where(past, att, 0.0)
        acc_ref[...] += _dot(vt_ref[0, j], att.astype(bf16))

    nd = tq // tk
    for d in range(nd - 1, -1, -1):
        step(i * nd + d, True)

    def body(jj, carry):
        step(i * nd - 1 - jj, False)
        return carry

    lax.fori_loop(0, i * nd, body, 0)
    o_ref[0] = _unpair_output(acc_ref[...], None, tq).astype(o_ref.dtype)


def _attn_sb(qt, k, vt):
    B, n1, S = qt.shape
    pairs = n1 // LANES
    tq, tk = TQ, TK_SB
    kern = functools.partial(_attn_sb_kernel, tq=tq, tk=tk)
    return pl.pallas_call(
        kern, grid=(B, pairs, S // tq),
        in_specs=[pl.BlockSpec((1, LANES, tq), lambda b, p, i: (b, p, i)),
                  pl.BlockSpec((1, S, LANES), lambda b, p, i: (b, 0, p)),
                  pl.BlockSpec((1, S // tk, LANES, tk), lambda b, p, i: (b, 0, p, 0))],
        out_specs=pl.BlockSpec((1, tq, LANES), lambda b, p, i: (b, i, p)),
        out_shape=jax.ShapeDtypeStruct((B, S, n1), bf16),
        scratch_shapes=[pltpu.VMEM((1, 2 * tq), f32), pltpu.VMEM((LANES, 2 * tq), f32)],
        compiler_params=_params("parallel", "parallel", "arbitrary"), name="attn_sb",
    )(qt, k, vt)


def _outproj_kernel(*refs, n_parts):
    o_refs = refs[:n_parts]
    w_refs = refs[n_parts:2 * n_parts]
    x_ref, g_ref, b_ref, y_ref = refs[2 * n_parts:]
    mix = _dot(o_refs[0][...], w_refs[0][...])
    for o_ref, w_ref in zip(o_refs[1:], w_refs[1:]):
        mix = mix + _dot(o_ref[...], w_ref[...])
    y_ref[...] = _layer_norm(DEEPNORM_ALPHA * x_ref[...] + mix, g_ref[...], b_ref[...])


def _outproj(parts, weights, x, g, b):
    M, D = x.shape
    tm = TOK_TILE
    n = len(parts)
    in_specs = [pl.BlockSpec((tm, o.shape[1]), lambda i: (i, 0)) for o in parts]
    in_specs += [_const_spec(w.shape) for w in weights]
    in_specs += [pl.BlockSpec((tm, D), lambda i: (i, 0)), _const_spec(g.shape), _const_spec(b.shape)]
    return pl.pallas_call(
        functools.partial(_outproj_kernel, n_parts=n), grid=(M // tm,),
        in_specs=in_specs, out_specs=pl.BlockSpec((tm, D), lambda i: (i, 0)),
        out_shape=jax.ShapeDtypeStruct((M, D), f32),
        compiler_params=_params("parallel"), name="outproj_ln",
    )(*parts, *weights, x, g, b)


def _ffn_kernel(x_ref, p_ref, w1_ref, w3_ref, w2_ref, g_ref, b_ref, wg_ref, bg_ref, wp_ref, y_ref):
    x = x_ref[...]
    xb = x.astype(bf16)
    h1 = _dot(xb, w1_ref[...])
    h3 = _dot(xb, w3_ref[...])
    act = (h1 * jax.nn.sigmoid(h1) * h3).astype(bf16)
    x2 = _layer_norm(DEEPNORM_ALPHA * x + _dot(act, w2_ref[...]), g_ref[...], b_ref[...])
    gate = jax.nn.sigmoid(_dot(x2.astype(bf16), wg_ref[...]) + bg_ref[...])
    y_ref[...] = x2 + gate * _dot(p_ref[...].astype(bf16), wp_ref[...])


def _ffn(x, p, w1, w3, w2, g, b, wg, bg, wp):
    M, D = x.shape
    tm = TOK_TILE
    consts = [w1, w3, w2, g, b, wg, bg, wp]
    in_specs = [pl.BlockSpec((tm, D), lambda i: (i, 0)), pl.BlockSpec((tm, p.shape[1]), lambda i: (i, 0))]
    in_specs += [_const_spec(c.shape) for c in consts]
    return pl.pallas_call(
        _ffn_kernel, grid=(M // tm,), in_specs=in_specs,
        out_specs=pl.BlockSpec((tm, D), lambda i: (i, 0)),
        out_shape=jax.ShapeDtypeStruct((M, D), f32),
        compiler_params=_params("parallel"), name="ffn_ln_ple",
    )(x, p, *consts)


def _proj_c_kernel(x_ref, wq_ref, wk_ref, wv_ref, wf_ref, bf_ref, qt_ref, k_ref, vt_ref, lf_ref):
    xb = x_ref[0].astype(bf16)
    qt_ref[0] = _dot_nt(wq_ref[...], xb).astype(bf16)
    k_ref[0] = _dot(xb, wk_ref[...]).astype(bf16)
    vt = _dot_nt(wv_ref[...], xb).astype(bf16)
    tm = vt.shape[1]
    for c in range(tm // TK_SOFTMAX):
        vt_ref[0, c] = vt[:, c * TK_SOFTMAX:(c + 1) * TK_SOFTMAX]
    lf_ref[0] = _log_sigmoid(_dot(xb, wf_ref[...]) + bf_ref[...])


def _proj_c(x, w):
    B, S, D = x.shape
    tm = TOK_TILE
    n = FOX_HEADS * HEAD_DIM
    consts = [w["q_t"], w["k"], w["v_t"], w["f"], w["bf"]]
    in_specs = [pl.BlockSpec((1, tm, D), lambda b, i: (b, i, 0))] + [_const_spec(c.shape) for c in consts]
    out_shape = [jax.ShapeDtypeStruct((B, n, S), bf16),
                 jax.ShapeDtypeStruct((B, S, n), bf16),
                 jax.ShapeDtypeStruct((B, S // TK_SOFTMAX, n, TK_SOFTMAX), bf16),
                 jax.ShapeDtypeStruct((B, S, LANES), f32)]
    out_specs = [pl.BlockSpec((1, n, tm), lambda b, i: (b, 0, i)),
                 pl.BlockSpec((1, tm, n), lambda b, i: (b, i, 0)),
                 pl.BlockSpec((1, tm // TK_SOFTMAX, n, TK_SOFTMAX), lambda b, i: (b, i, 0, 0)),
                 pl.BlockSpec((1, tm, LANES), lambda b, i: (b, i, 0))]
    return pl.pallas_call(
        _proj_c_kernel, grid=(B, S // tm), in_specs=in_specs, out_specs=out_specs, out_shape=out_shape,
        compiler_params=_params("parallel", "parallel"), name="proj_c",
    )(x, *consts)


CUM_BLOCK = 256


def _split3(v):
    hi = v.astype(bf16)
    r = v - hi.astype(f32)
    mid = r.astype(bf16)
    lo = (r - mid.astype(f32)).astype(bf16)
    return hi, mid, lo


def _fox_bias_kernel(lf_ref, selq_ref, selk_ref, qaug_ref, kaug_ref, carry_ref):
    n = CUM_BLOCK

    @pl.when(pl.program_id(1) == 0)
    def _():
        carry_ref[...] = jnp.zeros(carry_ref.shape, f32)

    lower = (lax.broadcasted_iota(jnp.int32, (n, n), 0) >= lax.broadcasted_iota(jnp.int32, (n, n), 1)).astype(bf16)
    lane = lax.broadcasted_iota(jnp.int32, (n, LANES), 1)
    hi, mid, lo = _split3(lf_ref[0])
    d = _dot(lower, hi) + _dot(lower, mid) + _dot(lower, lo) + carry_ref[...]
    carry_ref[...] = d[n - 1:n, :]
    pieces = [jnp.where(lane == LANES - 1, 1.0, v.astype(f32)).astype(bf16) for v in _split3(d)]
    qaug = _dot_nt(selq_ref[0], pieces[0]) + _dot_nt(selq_ref[1], pieces[1]) + _dot_nt(selq_ref[2], pieces[2])
    kaug = _dot(pieces[0], selk_ref[0]) + _dot(pieces[1], selk_ref[1]) + _dot(pieces[2], selk_ref[2])
    qaug_ref[0] = qaug.astype(bf16)
    kaug_ref[0] = kaug.astype(bf16)


def _fox_selectors():
    selq = np.zeros((3, FOX_HEADS * LANES, LANES), np.float32)
    selk = np.zeros((3, LANES, (FOX_HEADS // 2) * LANES), np.float32)
    one = LANES - 1
    for h in range(FOX_HEADS):
        pair, e = divmod(h, 2)
        for c in range(3):
            selq[c, h * LANES + 6 * e + c, h] = 1.0
            selk[c, h, pair * LANES + 6 * e + 3 + c] = -1.0
        for c in range(3):
            selq[0, h * LANES + 6 * e + 3 + c, one] = 1.0
            selk[0, one, pair * LANES + 6 * e + c] = 1.0
    return jnp.asarray(selq, bf16), jnp.asarray(selk, bf16)


def _fox_bias(lf):
    B, S, _ = lf.shape
    selq, selk = _fox_selectors()
    n = CUM_BLOCK
    nk = (FOX_HEADS // 2) * LANES
    return pl.pallas_call(
        _fox_bias_kernel, grid=(B, S // n),
        in_specs=[pl.BlockSpec((1, n, LANES), lambda b, j: (b, j, 0)), _const_spec(selq.shape), _const_spec(selk.shape)],
        out_specs=[pl.BlockSpec((1, FOX_HEADS * LANES, n), lambda b, j: (b, 0, j)),
                   pl.BlockSpec((1, n, nk), lambda b, j: (b, j, 0))],
        out_shape=[jax.ShapeDtypeStruct((B, FOX_HEADS * LANES, S), bf16),
                   jax.ShapeDtypeStruct((B, S, nk), bf16)],
        scratch_shapes=[pltpu.VMEM((1, LANES), f32)],
        compiler_params=_params("parallel", "arbitrary"), name="fox_bias",
    )(lf, selq, selk)


def _rotate_half_cols(w):
    half = w.shape[-1] // 2
    return jnp.concatenate([-w[..., half:], w[..., :half]], axis=-1)


def _pad_lanes(w):
    pad = [(0, 0)] * (w.ndim - 1) + [(0, LANES - w.shape[-1])]
    return jnp.pad(w, pad)


def _prep_layer_a(w_in, q_norm, w_uq, kv_norm, w_ukv):
    in_a = Q_LORA + KV_LORA + MLA_ROPE
    n = SB_HEADS * HEAD_DIM
    kr = w_in[:, Q_LORA + KV_LORA:in_a]
    sb = w_in[:, in_a:]
    uq = w_uq.reshape(Q_LORA, MLA_HEADS, MLA_NOPE + MLA_ROPE)
    rope = uq[:, :, MLA_NOPE:]
    ukv = w_ukv.reshape(KV_LORA, MLA_HEADS, MLA_NOPE + HEAD_DIM)
    return {
        "lat": w_in[:, :Q_LORA + KV_LORA].astype(bf16),
        "kr": jnp.concatenate([_pad_lanes(kr), _pad_lanes(_rotate_half_cols(kr))], axis=1).astype(bf16),
        "sbq_t": (sb[:, :n] * HEAD_SCALE).T.astype(bf16),
        "sbk": sb[:, n:2 * n].astype(bf16),
        "sbv_t": sb[:, 2 * n:].T.astype(bf16),
        "gq": q_norm.reshape(1, Q_LORA),
        "gkv": kv_norm.reshape(1, KV_LORA),
        "qn_t": uq[:, :, :MLA_NOPE].reshape(Q_LORA, -1).T.astype(bf16),
        "qra_t": _pad_lanes(rope).reshape(Q_LORA, -1).T.astype(bf16),
        "qrb_t": _pad_lanes(_rotate_half_cols(rope)).reshape(Q_LORA, -1).T.astype(bf16),
        "kn": ukv[:, :, :MLA_NOPE].reshape(KV_LORA, -1).astype(bf16),
        "v_t": ukv[:, :, MLA_NOPE:].reshape(KV_LORA, -1).T.astype(bf16),
    }


def _prep_layer_c(w_in, b_f):
    n = FOX_HEADS * HEAD_DIM
    return {
        "q_t": (w_in[:, :n] * HEAD_SCALE).T.astype(bf16),
        "k": w_in[:, n:2 * n].astype(bf16),
        "v_t": w_in[:, 2 * n:3 * n].T.astype(bf16),
        "f": _pad_lanes(w_in[:, 3 * n:]).astype(bf16),
        "bf": _pad_lanes(b_f.reshape(1, FOX_HEADS)),
    }


def _rope_tables(seq):
    inv = 1.0 / (ROPE_BASE ** (jnp.arange(0, MLA_ROPE, 2, dtype=f32) / MLA_ROPE))
    ang = jnp.arange(seq, dtype=f32)[:, None] * inv[None, :]
    cos = _pad_lanes(jnp.concatenate([jnp.cos(ang), jnp.cos(ang)], axis=1))
    sin = _pad_lanes(jnp.concatenate([jnp.sin(ang), jnp.sin(ang)], axis=1))
    return cos, sin, cos.T, sin.T


def kernel(x, p, a_w_in, a_q_norm, a_w_uq, a_kv_norm, a_w_ukv, a_w_out, c_w_in, c_b_f, c_w_out,
           ffn_w1, ffn_w3, ffn_w2, ln1_g, ln1_b, ln2_g, ln2_b, ple_w_proj, ple_w_gate, ple_b_gate):
    B, S, D = x.shape
    M = B * S
    row = lambda v: v.reshape(1, -1)

    def channel_mixer(x1, i):
        return _ffn(x1, p[i].reshape(M, P_DIM), ffn_w1[i].astype(bf16), ffn_w3[i].astype(bf16),
                    ffn_w2[i].astype(bf16), row(ln2_g[i]), row(ln2_b[i]), ple_w_gate[i].astype(bf16),
                    row(ple_b_gate[i]), ple_w_proj[i].astype(bf16))

    wa = _prep_layer_a(a_w_in[0], a_q_norm[0], a_w_uq[0], a_kv_norm[0], a_w_ukv[0])
    cos, sin, cost, sint = _rope_tables(S)
    qnt, qrt, kn, kr, vat, qbt, kb, vbt = _proj_a(x, wa, cos, sin, cost, sint)
    o_a = _attn_softmax(qnt, qrt, kn, kr, vat, kx_shared=True, chunk_shift=SEQ_CHUNK.bit_length() - 1, name="attn_mla")
    o_b = _attn_sb(qbt, kb, vbt)
    na = MLA_HEADS * HEAD_DIM
    w_out = a_w_out[0].astype(bf16)
    x1 = _outproj([o_a.reshape(M, -1), o_b.reshape(M, -1)], [w_out[:na], w_out[na:]], x.reshape(M, D),
                  row(ln1_g[0]), row(ln1_b[0]))
    x2 = channel_mixer(x1, 0)

    wc = _prep_layer_c(c_w_in[0], c_b_f[0])
    qt, k, vt, lf = _proj_c(x2.reshape(B, S, D), wc)
    qaug, kaug = _fox_bias(lf)
    o_c = _attn_softmax(qt, qaug, k, kaug, vt, kx_shared=False, chunk_shift=0, name="attn_fox")
    x3 = _outproj([o_c.reshape(M, -1)], [c_w_out[0].astype(bf16)], x2, row(ln1_g[1]), row(ln1_b[1]))
    x4 = channel_mixer(x3, 1)
    return x4.reshape(B, S, D)
```

```python
import functools

import jax
import jax.numpy as jnp
import numpy as np
from jax import lax
from jax.experimental import pallas as pl
from jax.experimental.pallas import tpu as pltpu

D_MODEL = 1024
SEQ_CHUNK = 64
P_DIM = 256
MLA_HEADS = 8
MLA_NOPE = 64
MLA_ROPE = 32
Q_LORA = 256
KV_LORA = 256
ROPE_BASE = 10000.0
SB_HEADS = 8
FOX_HEADS = 16
HEAD_DIM = 64
DEPTH = 2
DEEPNORM_ALPHA = (2.0 * DEPTH) ** 0.25
LOG2E = 1.4426950408889634
MLA_SCALE = (MLA_NOPE + MLA_ROPE) ** -0.5 * LOG2E
HEAD_SCALE = HEAD_DIM ** -0.5 * LOG2E

LANES = 128
BF16_ROWS = 16
VMEM_LIMIT = 56 * 1024 * 1024

TOK_TILE = 512
TQ = 256
PAIRS_PER_STEP = 4
TK_SOFTMAX = 256
TK_SB = 128
NEG = -1e30

bf16 = jnp.bfloat16
f32 = jnp.float32


def _dot(a, b):
    return jnp.dot(a, b, preferred_element_type=f32)


def _dot_nt(a, b):
    return lax.dot_general(a, b, (((1,), (1,)), ((), ())), preferred_element_type=f32)


def _layer_norm(z, g, b):
    mu = jnp.mean(z, axis=-1, keepdims=True)
    zc = z - mu
    var = jnp.mean(zc * zc, axis=-1, keepdims=True)
    return zc * lax.rsqrt(var + 1e-5) * g + b


def _rms_norm(c, g):
    return c * lax.rsqrt(jnp.mean(c * c, axis=-1, keepdims=True) + 1e-6) * g


def _log_sigmoid(a):
    return jnp.minimum(a, 0.0) - jnp.log(1.0 + jnp.exp(-jnp.abs(a)))


def _const_spec(shape):
    nd = len(shape)
    return pl.BlockSpec(shape, lambda *_: (0,) * nd, pipeline_mode=pl.Buffered(1))


def _params(*sem):
    return pltpu.CompilerParams(dimension_semantics=sem, vmem_limit_bytes=VMEM_LIMIT)


def _proj_a_kernel(x_ref, wlat_ref, wkr_ref, wsbq_ref, wsbk_ref, wsbv_ref, gq_ref, gkv_ref,
                   wqn_ref, wqra_ref, wqrb_ref, wkn_ref, wv_ref, cos_ref, sin_ref, cost_ref, sint_ref,
                   qnt_ref, qrt_ref, kn_ref, kr_ref, vat_ref, qbt_ref, kb_ref, vbt_ref):
    xb = x_ref[0].astype(bf16)
    lat = _dot(xb, wlat_ref[...])
    cq = _rms_norm(lat[:, :Q_LORA], gq_ref[...]).astype(bf16)
    ckv = _rms_norm(lat[:, Q_LORA:], gkv_ref[...]).astype(bf16)

    qnt_ref[0] = (_dot_nt(wqn_ref[...], cq) * MLA_SCALE).astype(bf16)
    qa = _dot_nt(wqra_ref[...], cq)
    qb = _dot_nt(wqrb_ref[...], cq)
    cost = cost_ref[...]
    sint = sint_ref[...]
    for h in range(MLA_HEADS):
        sl = slice(h * LANES, (h + 1) * LANES)
        qrt_ref[0, sl, :] = ((qa[sl] * cost + qb[sl] * sint) * MLA_SCALE).astype(bf16)

    kn_ref[0] = _dot(ckv, wkn_ref[...]).astype(bf16)
    kr2 = _dot(xb, wkr_ref[...])
    kr_ref[0] = (kr2[:, :LANES] * cos_ref[...] + kr2[:, LANES:] * sin_ref[...]).astype(bf16)

    vat = _dot_nt(wv_ref[...], ckv).astype(bf16)
    tm = vat.shape[1]
    for c in range(tm // TK_SOFTMAX):
        vat_ref[0, c] = vat[:, c * TK_SOFTMAX:(c + 1) * TK_SOFTMAX]

    qbt_ref[0] = (_dot_nt(wsbq_ref[...], xb) * HEAD_SCALE).astype(bf16)
    kb_ref[0] = _dot(xb, wsbk_ref[...]).astype(bf16)
    vbt = _dot_nt(wsbv_ref[...], xb).astype(bf16)
    for c in range(tm // TK_SB):
        vbt_ref[0, c] = vbt[:, c * TK_SB:(c + 1) * TK_SB]


def _proj_a(x, w, cos, sin, cost, sint):
    B, S, D = x.shape
    tm = TOK_TILE
    na = MLA_HEADS * HEAD_DIM
    nb = SB_HEADS * HEAD_DIM
    consts = [w["lat"], w["kr"], w["sbq_t"], w["sbk"], w["sbv_t"], w["gq"], w["gkv"],
              w["qn_t"], w["qra_t"], w["qrb_t"], w["kn"], w["v_t"]]
    in_specs = [pl.BlockSpec((1, tm, D), lambda b, i: (b, i, 0))]
    in_specs += [_const_spec(c.shape) for c in consts]
    in_specs += [pl.BlockSpec((tm, LANES), lambda b, i: (i, 0)),
                 pl.BlockSpec((tm, LANES), lambda b, i: (i, 0)),
                 pl.BlockSpec((LANES, tm), lambda b, i: (0, i)),
                 pl.BlockSpec((LANES, tm), lambda b, i: (0, i))]
    out_shape = [
        jax.ShapeDtypeStruct((B, na, S), bf16),
        jax.ShapeDtypeStruct((B, MLA_HEADS * LANES, S), bf16),
        jax.ShapeDtypeStruct((B, S, na), bf16),
        jax.ShapeDtypeStruct((B, S, LANES), bf16),
        jax.ShapeDtypeStruct((B, S // TK_SOFTMAX, na, TK_SOFTMAX), bf16),
        jax.ShapeDtypeStruct((B, nb, S), bf16),
        jax.ShapeDtypeStruct((B, S, nb), bf16),
        jax.ShapeDtypeStruct((B, S // TK_SB, nb, TK_SB), bf16),
    ]
    out_specs = [
        pl.BlockSpec((1, na, tm), lambda b, i: (b, 0, i)),
        pl.BlockSpec((1, MLA_HEADS * LANES, tm), lambda b, i: (b, 0, i)),
        pl.BlockSpec((1, tm, na), lambda b, i: (b, i, 0)),
        pl.BlockSpec((1, tm, LANES), lambda b, i: (b, i, 0)),
        pl.BlockSpec((1, tm // TK_SOFTMAX, na, TK_SOFTMAX), lambda b, i: (b, i, 0, 0)),
        pl.BlockSpec((1, nb, tm), lambda b, i: (b, 0, i)),
        pl.BlockSpec((1, tm, nb), lambda b, i: (b, i, 0)),
        pl.BlockSpec((1, tm // TK_SB, nb, TK_SB), lambda b, i: (b, i, 0, 0)),
    ]
    return pl.pallas_call(
        _proj_a_kernel, grid=(B, S // tm), in_specs=in_specs, out_specs=out_specs, out_shape=out_shape,
        compiler_params=_params("parallel", "parallel"), name="proj_a",
    )(x, *consts, cos, sin, cost, sint)


def _pair_queries(q1):
    row = lax.broadcasted_iota(jnp.int32, q1.shape, 0)
    zero = jnp.zeros_like(q1)
    return jnp.concatenate([jnp.where(row < HEAD_DIM, q1, zero), jnp.where(row >= HEAD_DIM, q1, zero)], axis=1)


def _positions(shape, key_start, query_start, tq):
    kpos = key_start + lax.broadcasted_iota(jnp.int32, shape, 0)
    col = lax.broadcasted_iota(jnp.int32, shape, 1)
    qpos = query_start + jnp.where(col >= tq, col - tq, col)
    return kpos, qpos


def _unpair_output(acc, inv, tq):
    if inv is not None:
        acc = acc * inv
    ot = jnp.concatenate([acc[:HEAD_DIM, :tq], acc[HEAD_DIM:2 * HEAD_DIM, tq:]], axis=0)
    return ot.T


def _tile(c):
    return slice(c * LANES, (c + 1) * LANES)


def _attn_softmax_kernel(q1_ref, qx_ref, k1_ref, kx_ref, vt_ref, o_ref, q2t_ref, m_ref, acc_ref,
                         *, tq, tk, chunk_shift, kx_shared):
    i = pl.program_id(2)
    chains = range(PAIRS_PER_STEP)
    for c in chains:
        qx = jnp.concatenate([qx_ref[0, _tile(2 * c), :], qx_ref[0, _tile(2 * c + 1), :]], axis=1)
        q2t_ref[c] = jnp.concatenate([_pair_queries(q1_ref[0, _tile(c), :]), qx], axis=0)
    m_ref[...] = jnp.full(m_ref.shape, NEG, f32)
    acc_ref[...] = jnp.zeros(acc_ref.shape, f32)
    ones = jnp.ones((BF16_ROWS, tk), bf16)

    def step(j, masked):
        ks = pl.multiple_of(j * tk, tk)
        s = []
        for c in chains:
            kx = kx_ref[0, pl.ds(ks, tk), _tile(0 if kx_shared else c)]
            kb = jnp.concatenate([k1_ref[0, pl.ds(ks, tk), _tile(c)], kx], axis=1)
            s.append(_dot(kb, q2t_ref[c]))
        if masked:
            kpos, qpos = _positions(s[0].shape, ks, i * tq, tq)
            ok = (kpos >> chunk_shift) <= (qpos >> chunk_shift)
            s = [jnp.where(ok, sc, NEG) for sc in s]
        for c in chains:
            m_old = m_ref[c]
            m_new = jnp.maximum(m_old, jnp.max(s[c], axis=0, keepdims=True))
            alpha = jnp.exp2(m_old - m_new)
            p = jnp.exp2(s[c] - m_new).astype(bf16)
            vta = jnp.concatenate([vt_ref[0, j, _tile(c), :], ones], axis=0)
            acc_ref[c] = alpha * acc_ref[c] + _dot(vta, p)
            m_ref[c] = m_new

    def body(j, carry):
        step(j, False)
        return carry

    lax.fori_loop(0, i, body, 0)
    step(i, True)
    for c in chains:
        acc = acc_ref[c]
        inv = 1.0 / acc[LANES:LANES + 1, :]
        o_ref[0, :, _tile(c)] = _unpair_output(acc, inv, tq).astype(o_ref.dtype)


def _attn_softmax(q1t, qxt, k1, kx, vt, *, kx_shared, chunk_shift, name):
    B, n1, S = q1t.shape
    C = PAIRS_PER_STEP
    groups = n1 // (C * LANES)
    tq, tk = TQ, TK_SOFTMAX
    assert tq == tk
    if kx_shared:
        kx_spec = pl.BlockSpec((1, S, LANES), lambda b, g, i: (b, 0, 0))
    else:
        kx_spec = pl.BlockSpec((1, S, C * LANES), lambda b, g, i: (b, 0, g))
    kern = functools.partial(_attn_softmax_kernel, tq=tq, tk=tk, chunk_shift=chunk_shift, kx_shared=kx_shared)
    return pl.pallas_call(
        kern, grid=(B, groups, S // tq),
        in_specs=[pl.BlockSpec((1, C * LANES, tq), lambda b, g, i: (b, g, i)),
                  pl.BlockSpec((1, 2 * C * LANES, tq), lambda b, g, i: (b, g, i)),
                  pl.BlockSpec((1, S, C * LANES), lambda b, g, i: (b, 0, g)),
                  kx_spec,
                  pl.BlockSpec((1, S // tk, C * LANES, tk), lambda b, g, i: (b, 0, g, 0))],
        out_specs=pl.BlockSpec((1, tq, C * LANES), lambda b, g, i: (b, i, g)),
        out_shape=jax.ShapeDtypeStruct((B, S, n1), bf16),
        scratch_shapes=[pltpu.VMEM((C, 2 * LANES, 2 * tq), bf16),
                        pltpu.VMEM((C, 1, 2 * tq), f32),
                        pltpu.VMEM((C, LANES + BF16_ROWS, 2 * tq), f32)],
        compiler_params=_params("parallel", "parallel", "arbitrary"), name=name,
    )(q1t, qxt, k1, kx, vt)


def _attn_sb_kernel(qt_ref, k_ref, vt_ref, o_ref, q2t_ref, r_ref, acc_ref, *, tq, tk):
    i = pl.program_id(2)
    chains = range(PAIRS_PER_STEP)
    for c in chains:
        q2t_ref[c] = _pair_queries(qt_ref[0, _tile(c), :])
    r_ref[...] = jnp.zeros(r_ref.shape, f32)
    acc_ref[...] = jnp.zeros(acc_ref.shape, f32)
    tri = (lax.broadcasted_iota(jnp.int32, (tk, tk), 0) < lax.broadcasted_iota(jnp.int32, (tk, tk), 1)).astype(bf16)
    tri2 = jnp.concatenate([tri, tri], axis=1)

    def step(j, masked):
        ks = pl.multiple_of(j * tk, tk)
        z = [_dot(k_ref[0, pl.ds(ks, tk), _tile(c)], q2t_ref[c]) for c in chains]
        if masked:
            kpos, qpos = _positions(z[0].shape, ks, i * tq, tq)
            past = kpos < qpos
        for c in chains:
            log_beta = jnp.minimum(z[c], 0.0) - jnp.log2(1.0 + jnp.exp2(-jnp.abs(z[c])))
            log_rem = log_beta - z[c]
            if masked:
                log_rem = jnp.where(past, log_rem, 0.0)
            hi = log_rem.astype(bf16)
            lo = (log_rem - hi.astype(f32)).astype(bf16)
            between = _dot(tri2, jnp.concatenate([hi, lo], axis=0)) + r_ref[c]
            r_ref[c] = between[0:1, :] + log_rem[0:1, :]
            att = jnp.exp2(log_beta + between)
            if masked:
                att = jnp.where(past, att, 0.0)
            acc_ref[c] += _dot(vt_ref[0, j, _tile(c), :], att.astype(bf16))

    nd = tq // tk
    for d in range(nd - 1, -1, -1):
        step(i * nd + d, True)

    def body(jj, carry):
        step(i * nd - 1 - jj, False)
        return carry

    lax.fori_loop(0, i * nd, body, 0)
    for c in chains:
        o_ref[0, :, _tile(c)] = _unpair_output(acc_ref[c], None, tq).astype(o_ref.dtype)


def _attn_sb(qt, k, vt):
    B, n1, S = qt.shape
    C = PAIRS_PER_STEP
    groups = n1 // (C * LANES)
    tq, tk = TQ, TK_SB
    kern = functools.partial(_attn_sb_kernel, tq=tq, tk=tk)
    return pl.pallas_call(
        kern, grid=(B, groups, S // tq),
        in_specs=[pl.BlockSpec((1, C * LANES, tq), lambda b, g, i: (b, g, i)),
                  pl.BlockSpec((1, S, C * LANES), lambda b, g, i: (b, 0, g)),
                  pl.BlockSpec((1, S // tk, C * LANES, tk), lambda b, g, i: (b, 0, g, 0))],
        out_specs=pl.BlockSpec((1, tq, C * LANES), lambda b, g, i: (b, i, g)),
        out_shape=jax.ShapeDtypeStruct((B, S, n1), bf16),
        scratch_shapes=[pltpu.VMEM((C, LANES, 2 * tq), bf16),
                        pltpu.VMEM((C, 1, 2 * tq), f32),
                        pltpu.VMEM((C, LANES, 2 * tq), f32)],
        compiler_params=_params("parallel", "parallel", "arbitrary"), name="attn_sb",
    )(qt, k, vt)


def _outproj_kernel(*refs, n_parts):
    o_refs = refs[:n_parts]
    w_refs = refs[n_parts:2 * n_parts]
    x_ref, g_ref, b_ref, y_ref = refs[2 * n_parts:]
    mix = _dot(o_refs[0][...], w_refs[0][...])
    for o_ref, w_ref in zip(o_refs[1:], w_refs[1:]):
        mix = mix + _dot(o_ref[...], w_ref[...])
    y_ref[...] = _layer_norm(DEEPNORM_ALPHA * x_ref[...] + mix, g_ref[...], b_ref[...])


def _outproj(parts, weights, x, g, b):
    M, D = x.shape
    tm = TOK_TILE
    n = len(parts)
    in_specs = [pl.BlockSpec((tm, o.shape[1]), lambda i: (i, 0)) for o in parts]
    in_specs += [_const_spec(w.shape) for w in weights]
    in_specs += [pl.BlockSpec((tm, D), lambda i: (i, 0)), _const_spec(g.shape), _const_spec(b.shape)]
    return pl.pallas_call(
        functools.partial(_outproj_kernel, n_parts=n), grid=(M // tm,),
        in_specs=in_specs, out_specs=pl.BlockSpec((tm, D), lambda i: (i, 0)),
        out_shape=jax.ShapeDtypeStruct((M, D), f32),
        compiler_params=_params("parallel"), name="outproj_ln",
    )(*parts, *weights, x, g, b)


def _ffn_kernel(x_ref, p_ref, w1_ref, w3_ref, w2_ref, g_ref, b_ref, wg_ref, bg_ref, wp_ref, y_ref):
    x = x_ref[...]
    xb = x.astype(bf16)
    h1 = _dot(xb, w1_ref[...])
    h3 = _dot(xb, w3_ref[...])
    act = (h1 * jax.nn.sigmoid(h1) * h3).astype(bf16)
    x2 = _layer_norm(DEEPNORM_ALPHA * x + _dot(act, w2_ref[...]), g_ref[...], b_ref[...])
    gate = jax.nn.sigmoid(_dot(x2.astype(bf16), wg_ref[...]) + bg_ref[...])
    y_ref[...] = x2 + gate * _dot(p_ref[...].astype(bf16), wp_ref[...])


def _ffn(x, p, w1, w3, w2, g, b, wg, bg, wp):
    M, D = x.shape
    tm = TOK_TILE
    consts = [w1, w3, w2, g, b, wg, bg, wp]
    in_specs = [pl.BlockSpec((tm, D), lambda i: (i, 0)), pl.BlockSpec((tm, p.shape[1]), lambda i: (i, 0))]
    in_specs += [_const_spec(c.shape) for c in consts]
    return pl.pallas_call(
        _ffn_kernel, grid=(M // tm,), in_specs=in_specs,
        out_specs=pl.BlockSpec((tm, D), lambda i: (i, 0)),
        out_shape=jax.ShapeDtypeStruct((M, D), f32),
        compiler_params=_params("parallel"), name="ffn_ln_ple",
    )(x, p, *consts)


def _proj_c_kernel(x_ref, wq_ref, wk_ref, wv_ref, wf_ref, bf_ref, qt_ref, k_ref, vt_ref, lf_ref):
    xb = x_ref[0].astype(bf16)
    qt_ref[0] = (_dot_nt(wq_ref[...], xb) * HEAD_SCALE).astype(bf16)
    k_ref[0] = _dot(xb, wk_ref[...]).astype(bf16)
    vt = _dot_nt(wv_ref[...], xb).astype(bf16)
    tm = vt.shape[1]
    for c in range(tm // TK_SOFTMAX):
        vt_ref[0, c] = vt[:, c * TK_SOFTMAX:(c + 1) * TK_SOFTMAX]
    lf_ref[0] = _log_sigmoid(_dot(xb, wf_ref[...]) + bf_ref[...])


def _proj_c(x, w):
    B, S, D = x.shape
    tm = TOK_TILE
    n = FOX_HEADS * HEAD_DIM
    consts = [w["q_t"], w["k"], w["v_t"], w["f"], w["bf"]]
    in_specs = [pl.BlockSpec((1, tm, D), lambda b, i: (b, i, 0))] + [_const_spec(c.shape) for c in consts]
    out_shape = [jax.ShapeDtypeStruct((B, n, S), bf16),
                 jax.ShapeDtypeStruct((B, S, n), bf16),
                 jax.ShapeDtypeStruct((B, S // TK_SOFTMAX, n, TK_SOFTMAX), bf16),
                 jax.ShapeDtypeStruct((B, S, LANES), f32)]
    out_specs = [pl.BlockSpec((1, n, tm), lambda b, i: (b, 0, i)),
                 pl.BlockSpec((1, tm, n), lambda b, i: (b, i, 0)),
                 pl.BlockSpec((1, tm // TK_SOFTMAX, n, TK_SOFTMAX), lambda b, i: (b, i, 0, 0)),
                 pl.BlockSpec((1, tm, LANES), lambda b, i: (b, i, 0))]
    return pl.pallas_call(
        _proj_c_kernel, grid=(B, S // tm), in_specs=in_specs, out_specs=out_specs, out_shape=out_shape,
        compiler_params=_params("parallel", "parallel"), name="proj_c",
    )(x, *consts)


CUM_BLOCK = 256


def _split3(v):
    hi = v.astype(bf16)
    r = v - hi.astype(f32)
    mid = r.astype(bf16)
    lo = (r - mid.astype(f32)).astype(bf16)
    return hi, mid, lo


def _fox_bias_kernel(lf_ref, selq_ref, selk_ref, qaug_ref, kaug_ref, carry_ref):
    n = CUM_BLOCK

    @pl.when(pl.program_id(1) == 0)
    def _():
        carry_ref[...] = jnp.zeros(carry_ref.shape, f32)

    lower = (lax.broadcasted_iota(jnp.int32, (n, n), 0) >= lax.broadcasted_iota(jnp.int32, (n, n), 1)).astype(bf16)
    lane = lax.broadcasted_iota(jnp.int32, (n, LANES), 1)
    hi, mid, lo = _split3(lf_ref[0])
    d = _dot(lower, hi) + _dot(lower, mid) + _dot(lower, lo) + carry_ref[...]
    carry_ref[...] = d[n - 1:n, :]
    pieces = [jnp.where(lane == LANES - 1, 1.0, v.astype(f32)).astype(bf16) for v in _split3(d * LOG2E)]
    qaug = _dot_nt(selq_ref[0], pieces[0]) + _dot_nt(selq_ref[1], pieces[1]) + _dot_nt(selq_ref[2], pieces[2])
    kaug = _dot(pieces[0], selk_ref[0]) + _dot(pieces[1], selk_ref[1]) + _dot(pieces[2], selk_ref[2])
    qaug_ref[0] = qaug.astype(bf16)
    kaug_ref[0] = kaug.astype(bf16)


def _fox_selectors():
    selq = np.zeros((3, FOX_HEADS * LANES, LANES), np.float32)
    selk = np.zeros((3, LANES, (FOX_HEADS // 2) * LANES), np.float32)
    one = LANES - 1
    for h in range(FOX_HEADS):
        pair, e = divmod(h, 2)
        for c in range(3):
            selq[c, h * LANES + 6 * e + c, h] = 1.0
            selk[c, h, pair * LANES + 6 * e + 3 + c] = -1.0
        for c in range(3):
            selq[0, h * LANES + 6 * e + 3 + c, one] = 1.0
            selk[0, one, pair * LANES + 6 * e + c] = 1.0
    return jnp.asarray(selq, bf16), jnp.asarray(selk, bf16)


def _fox_bias(lf):
    B, S, _ = lf.shape
    selq, selk = _fox_selectors()
    n = CUM_BLOCK
    nk = (FOX_HEADS // 2) * LANES
    return pl.pallas_call(
        _fox_bias_kernel, grid=(B, S // n),
        in_specs=[pl.BlockSpec((1, n, LANES), lambda b, j: (b, j, 0)), _const_spec(selq.shape), _const_spec(selk.shape)],
        out_specs=[pl.BlockSpec((1, FOX_HEADS * LANES, n), lambda b, j: (b, 0, j)),
                   pl.BlockSpec((1, n, nk), lambda b, j: (b, j, 0))],
        out_shape=[jax.ShapeDtypeStruct((B, FOX_HEADS * LANES, S), bf16),
                   jax.ShapeDtypeStruct((B, S, nk), bf16)],
        scratch_shapes=[pltpu.VMEM((1, LANES), f32)],
        compiler_params=_params("parallel", "arbitrary"), name="fox_bias",
    )(lf, selq, selk)


def _rotate_half_cols(w):
    half = w.shape[-1] // 2
    return jnp.concatenate([-w[..., half:], w[..., :half]], axis=-1)


def _pad_lanes(w):
    pad = [(0, 0)] * (w.ndim - 1) + [(0, LANES - w.shape[-1])]
    return jnp.pad(w, pad)


def _prep_layer_a(w_in, q_norm, w_uq, kv_norm, w_ukv):
    in_a = Q_LORA + KV_LORA + MLA_ROPE
    n = SB_HEADS * HEAD_DIM
    kr = w_in[:, Q_LORA + KV_LORA:in_a]
    sb = w_in[:, in_a:]
    uq = w_uq.reshape(Q_LORA, MLA_HEADS, MLA_NOPE + MLA_ROPE)
    rope = uq[:, :, MLA_NOPE:]
    ukv = w_ukv.reshape(KV_LORA, MLA_HEADS, MLA_NOPE + HEAD_DIM)
    return {
        "lat": w_in[:, :Q_LORA + KV_LORA].astype(bf16),
        "kr": jnp.concatenate([_pad_lanes(kr), _pad_lanes(_rotate_half_cols(kr))], axis=1).astype(bf16),
        "sbq_t": sb[:, :n].T.astype(bf16),
        "sbk": sb[:, n:2 * n].astype(bf16),
        "sbv_t": sb[:, 2 * n:].T.astype(bf16),
        "gq": q_norm.reshape(1, Q_LORA),
        "gkv": kv_norm.reshape(1, KV_LORA),
        "qn_t": uq[:, :, :MLA_NOPE].reshape(Q_LORA, -1).T.astype(bf16),
        "qra_t": _pad_lanes(rope).reshape(Q_LORA, -1).T.astype(bf16),
        "qrb_t": _pad_lanes(_rotate_half_cols(rope)).reshape(Q_LORA, -1).T.astype(bf16),
        "kn": ukv[:, :, :MLA_NOPE].reshape(KV_LORA, -1).astype(bf16),
        "v_t": ukv[:, :, MLA_NOPE:].reshape(KV_LORA, -1).T.astype(bf16),
    }


def _prep_layer_c(w_in, b_f):
    n = FOX_HEADS * HEAD_DIM
    return {
        "q_t": w_in[:, :n].T.astype(bf16),
        "k": w_in[:, n:2 * n].astype(bf16),
        "v_t": w_in[:, 2 * n:3 * n].T.astype(bf16),
        "f": _pad_lanes(w_in[:, 3 * n:]).astype(bf16),
        "bf": _pad_lanes(b_f.reshape(1, FOX_HEADS)),
    }


def _rope_tables(seq):
    inv = 1.0 / (ROPE_BASE ** (jnp.arange(0, MLA_ROPE, 2, dtype=f32) / MLA_ROPE))
    ang = jnp.arange(seq, dtype=f32)[:, None] * inv[None, :]
    cos = _pad_lanes(jnp.concatenate([jnp.cos(ang), jnp.cos(ang)], axis=1))
    sin = _pad_lanes(jnp.concatenate([jnp.sin(ang), jnp.sin(ang)], axis=1))
    return cos, sin, cos.T, sin.T


def kernel(x, p, a_w_in, a_q_norm, a_w_uq, a_kv_norm, a_w_ukv, a_w_out, c_w_in, c_b_f, c_w_out,
           ffn_w1, ffn_w3, ffn_w2, ln1_g, ln1_b, ln2_g, ln2_b, ple_w_proj, ple_w_gate, ple_b_gate):
    B, S, D = x.shape
    M = B * S
    row = lambda v: v.reshape(1, -1)

    def channel_mixer(x1, i):
        return _ffn(x1, p[i].reshape(M, P_DIM), ffn_w1[i].astype(bf16), ffn_w3[i].astype(bf16),
                    ffn_w2[i].astype(bf16), row(ln2_g[i]), row(ln2_b[i]), ple_w_gate[i].astype(bf16),
                    row(ple_b_gate[i]), ple_w_proj[i].astype(bf16))

    wa = _prep_layer_a(a_w_in[0], a_q_norm[0], a_w_uq[0], a_kv_norm[0], a_w_ukv[0])
    cos, sin, cost, sint = _rope_tables(S)
    qnt, qrt, kn, kr, vat, qbt, kb, vbt = _proj_a(x, wa, cos, sin, cost, sint)
    o_a = _attn_softmax(qnt, qrt, kn, kr, vat, kx_shared=True, chunk_shift=SEQ_CHUNK.bit_length() - 1, name="attn_mla")
    o_b = _attn_sb(qbt, kb, vbt)
    na = MLA_HEADS * HEAD_DIM
    w_out = a_w_out[0].astype(bf16)
    x1 = _outproj([o_a.reshape(M, -1), o_b.reshape(M, -1)], [w_out[:na], w_out[na:]], x.reshape(M, D),
                  row(ln1_g[0]), row(ln1_b[0]))
    x2 = channel_mixer(x1, 0)

    wc = _prep_layer_c(c_w_in[0], c_b_f[0])
    qt, k, vt, lf = _proj_c(x2.reshape(B, S, D), wc)
    qaug, kaug = _fox_bias(lf)
    o_c = _attn_softmax(qt, qaug, k, kaug, vt, kx_shared=False, chunk_shift=0, name="attn_fox")
    x3 = _outproj([o_c.reshape(M, -1)], [c_w_out[0].astype(bf16)], x2, row(ln1_g[1]), row(ln1_b[1]))
    x4 = channel_mixer(x3, 1)
    return x4.reshape(B, S, D)
```

```python
import functools

import jax
import jax.numpy as jnp
import numpy as np
from jax import lax
from jax.experimental import pallas as pl
from jax.experimental.pallas import tpu as pltpu

D_MODEL = 1024
SEQ_CHUNK = 64
P_DIM = 256
MLA_HEADS = 8
MLA_NOPE = 64
MLA_ROPE = 32
Q_LORA = 256
KV_LORA = 256
ROPE_BASE = 10000.0
SB_HEADS = 8
FOX_HEADS = 16
HEAD_DIM = 64
DEPTH = 2
DEEPNORM_ALPHA = (2.0 * DEPTH) ** 0.25
LOG2E = 1.4426950408889634
MLA_SCALE = (MLA_NOPE + MLA_ROPE) ** -0.5 * LOG2E
HEAD_SCALE = HEAD_DIM ** -0.5 * LOG2E

LANES = 128
BF16_ROWS = 16
VMEM_LIMIT = 56 * 1024 * 1024

TOK_TILE = 512
TQ = 256
PAIRS_PER_STEP = 4
TK_SOFTMAX = 256
TK_SB = 128
NEG = -1e30

bf16 = jnp.bfloat16
f32 = jnp.float32


def _dot(a, b):
    return jnp.dot(a, b, preferred_element_type=f32)


def _dot_nt(a, b):
    return lax.dot_general(a, b, (((1,), (1,)), ((), ())), preferred_element_type=f32)


def _layer_norm(z, g, b):
    mu = jnp.mean(z, axis=-1, keepdims=True)
    zc = z - mu
    var = jnp.mean(zc * zc, axis=-1, keepdims=True)
    return zc * lax.rsqrt(var + 1e-5) * g + b


def _rms_norm(c, g):
    return c * lax.rsqrt(jnp.mean(c * c, axis=-1, keepdims=True) + 1e-6) * g


def _log_sigmoid(a):
    return jnp.minimum(a, 0.0) - jnp.log(1.0 + jnp.exp(-jnp.abs(a)))


def _const_spec(shape):
    nd = len(shape)
    return pl.BlockSpec(shape, lambda *_: (0,) * nd, pipeline_mode=pl.Buffered(1))


def _params(*sem):
    return pltpu.CompilerParams(dimension_semantics=sem, vmem_limit_bytes=VMEM_LIMIT)


def _proj_a_kernel(x_ref, wlat_ref, wkr_ref, wsbq_ref, wsbk_ref, wsbv_ref, gq_ref, gkv_ref,
                   wqn_ref, wqra_ref, wqrb_ref, wkn_ref, wv_ref, cos_ref, sin_ref, cost_ref, sint_ref,
                   qnt_ref, qrt_ref, kn_ref, kr_ref, vat_ref, qbt_ref, kb_ref, vbt_ref):
    xb = x_ref[0].astype(bf16)
    lat = _dot(xb, wlat_ref[...])
    cq = _rms_norm(lat[:, :Q_LORA], gq_ref[...]).astype(bf16)
    ckv = _rms_norm(lat[:, Q_LORA:], gkv_ref[...]).astype(bf16)

    qnt_ref[0] = (_dot_nt(wqn_ref[...], cq) * MLA_SCALE).astype(bf16)
    qa = _dot_nt(wqra_ref[...], cq)
    qb = _dot_nt(wqrb_ref[...], cq)
    cost = cost_ref[...]
    sint = sint_ref[...]
    for h in range(MLA_HEADS):
        sl = slice(h * LANES, (h + 1) * LANES)
        qrt_ref[0, sl, :] = ((qa[sl] * cost + qb[sl] * sint) * MLA_SCALE).astype(bf16)

    kn_ref[0] = _dot(ckv, wkn_ref[...]).astype(bf16)
    kr2 = _dot(xb, wkr_ref[...])
    kr_ref[0] = (kr2[:, :LANES] * cos_ref[...] + kr2[:, LANES:] * sin_ref[...]).astype(bf16)

    vat = _dot_nt(wv_ref[...], ckv).astype(bf16)
    tm = vat.shape[1]
    for c in range(tm // TK_SOFTMAX):
        vat_ref[0, c] = vat[:, c * TK_SOFTMAX:(c + 1) * TK_SOFTMAX]

    qbt_ref[0] = (_dot_nt(wsbq_ref[...], xb) * HEAD_SCALE).astype(bf16)
    kb_ref[0] = _dot(xb, wsbk_ref[...]).astype(bf16)
    vbt = _dot_nt(wsbv_ref[...], xb).astype(bf16)
    for c in range(tm // TK_SB):
        vbt_ref[0, c] = vbt[:, c * TK_SB:(c + 1) * TK_SB]


def _proj_a(x, w, cos, sin, cost, sint):
    B, S, D = x.shape
    tm = TOK_TILE
    na = MLA_HEADS * HEAD_DIM
    nb = SB_HEADS * HEAD_DIM
    consts = [w["lat"], w["kr"], w["sbq_t"], w["sbk"], w["sbv_t"], w["gq"], w["gkv"],
              w["qn_t"], w["qra_t"], w["qrb_t"], w["kn"], w["v_t"]]
    in_specs = [pl.BlockSpec((1, tm, D), lambda b, i: (b, i, 0))]
    in_specs += [_const_spec(c.shape) for c in consts]
    in_specs += [pl.BlockSpec((tm, LANES), lambda b, i: (i, 0)),
                 pl.BlockSpec((tm, LANES), lambda b, i: (i, 0)),
                 pl.BlockSpec((LANES, tm), lambda b, i: (0, i)),
                 pl.BlockSpec((LANES, tm), lambda b, i: (0, i))]
    out_shape = [
        jax.ShapeDtypeStruct((B, na, S), bf16),
        jax.ShapeDtypeStruct((B, MLA_HEADS * LANES, S), bf16),
        jax.ShapeDtypeStruct((B, S, na), bf16),
        jax.ShapeDtypeStruct((B, S, LANES), bf16),
        jax.ShapeDtypeStruct((B, S // TK_SOFTMAX, na, TK_SOFTMAX), bf16),
        jax.ShapeDtypeStruct((B, nb, S), bf16),
        jax.ShapeDtypeStruct((B, S, nb), bf16),
        jax.ShapeDtypeStruct((B, S // TK_SB, nb, TK_SB), bf16),
    ]
    out_specs = [
        pl.BlockSpec((1, na, tm), lambda b, i: (b, 0, i)),
        pl.BlockSpec((1, MLA_HEADS * LANES, tm), lambda b, i: (b, 0, i)),
        pl.BlockSpec((1, tm, na), lambda b, i: (b, i, 0)),
        pl.BlockSpec((1, tm, LANES), lambda b, i: (b, i, 0)),
        pl.BlockSpec((1, tm // TK_SOFTMAX, na, TK_SOFTMAX), lambda b, i: (b, i, 0, 0)),
        pl.BlockSpec((1, nb, tm), lambda b, i: (b, 0, i)),
        pl.BlockSpec((1, tm, nb), lambda b, i: (b, i, 0)),
        pl.BlockSpec((1, tm // TK_SB, nb, TK_SB), lambda b, i: (b, i, 0, 0)),
    ]
    return pl.pallas_call(
        _proj_a_kernel, grid=(B, S // tm), in_specs=in_specs, out_specs=out_specs, out_shape=out_shape,
        compiler_params=_params("parallel", "parallel"), name="proj_a",
    )(x, *consts, cos, sin, cost, sint)


def _pair_queries(q1):
    row = lax.broadcasted_iota(jnp.int32, q1.shape, 0)
    zero = jnp.zeros_like(q1)
    return jnp.concatenate([jnp.where(row < HEAD_DIM, q1, zero), jnp.where(row >= HEAD_DIM, q1, zero)], axis=1)


def _positions(shape, key_start, query_start, tq):
    kpos = key_start + lax.broadcasted_iota(jnp.int32, shape, 0)
    col = lax.broadcasted_iota(jnp.int32, shape, 1)
    qpos = query_start + jnp.where(col >= tq, col - tq, col)
    return kpos, qpos


def _unpair_output(acc, inv, tq):
    if inv is not None:
        acc = acc * inv
    ot = jnp.concatenate([acc[:HEAD_DIM, :tq], acc[HEAD_DIM:2 * HEAD_DIM, tq:]], axis=0)
    return ot.T


def _tile(c):
    return slice(c * LANES, (c + 1) * LANES)


def _chain_scratch(refs, per_chain):
    return [refs[c * per_chain:(c + 1) * per_chain] for c in range(PAIRS_PER_STEP)]


def _attn_softmax_kernel(q1_ref, qx_ref, k1_ref, kx_ref, vt_ref, o_ref, *scratch, tq, tk, chunk_shift, kx_shared):
    i = pl.program_id(2)
    chains = range(PAIRS_PER_STEP)
    q2t, m, acc0, acc1, sa, sb = zip(*_chain_scratch(scratch, 6))
    acc = (acc0, acc1)
    for c in chains:
        qx = jnp.concatenate([qx_ref[0, _tile(2 * c), :], qx_ref[0, _tile(2 * c + 1), :]], axis=1)
        q2t[c][...] = jnp.concatenate([_pair_queries(q1_ref[0, _tile(c), :]), qx], axis=0)
        m[c][...] = jnp.full(m[c].shape, NEG, f32)
        for h in range(2):
            acc[h][c][...] = jnp.zeros(acc[h][c].shape, f32)
    ones = jnp.ones((BF16_ROWS, tk), bf16)

    def scores(j, s_refs):
        ks = pl.multiple_of(j * tk, tk)
        for c in chains:
            kx = kx_ref[0, pl.ds(ks, tk), _tile(0 if kx_shared else c)]
            kb = jnp.concatenate([k1_ref[0, pl.ds(ks, tk), _tile(c)], kx], axis=1)
            s_refs[c][...] = _dot(kb, q2t[c][...])

    def accumulate(j, s_refs, masked, j_next=None, next_refs=None):
        s = [s_refs[c][...] for c in chains]
        if masked:
            kpos, qpos = _positions(s[0].shape, j * tk, i * tq, tq)
            ok = (kpos >> chunk_shift) <= (qpos >> chunk_shift)
            s = [jnp.where(ok, sc, NEG) for sc in s]
        m_new, alpha = [], []
        for c in chains:
            m_old = m[c][...]
            m_new.append(jnp.maximum(m_old, jnp.max(s[c], axis=0, keepdims=True)))
            alpha.append(jnp.exp2(m_old - m_new[c]))
            m[c][...] = m_new[c]
        if next_refs is not None:
            scores(j_next, next_refs)
        for c in chains:
            p = jnp.exp2(s[c] - m_new[c]).astype(bf16)
            for h in range(2):
                vta = jnp.concatenate([vt_ref[0, j, pl.ds(c * LANES + h * HEAD_DIM, HEAD_DIM), :], ones], axis=0)
                cols = slice(h * tq, (h + 1) * tq)
                acc[h][c][...] = alpha[c][:, cols] * acc[h][c][...] + _dot(vta, p[:, cols])

    scores(0, sa)

    def body(u, carry):
        accumulate(2 * u, sa, False, 2 * u + 1, sb)
        accumulate(2 * u + 1, sb, False, 2 * u + 2, sa)
        return carry

    lax.fori_loop(0, i // 2, body, 0)

    @pl.when(i % 2 == 0)
    def _():
        accumulate(i, sa, True)

    @pl.when(i % 2 == 1)
    def _():
        accumulate(i - 1, sa, False, i, sb)
        accumulate(i, sb, True)

    for c in chains:
        heads = []
        for h in range(2):
            a = acc[h][c][...]
            heads.append(a[:HEAD_DIM] * (1.0 / a[HEAD_DIM:HEAD_DIM + 1]))
        o_ref[0, :, _tile(c)] = jnp.concatenate(heads, axis=0).T.astype(o_ref.dtype)


def _attn_softmax(q1t, qxt, k1, kx, vt, *, kx_shared, chunk_shift, name):
    B, n1, S = q1t.shape
    C = PAIRS_PER_STEP
    groups = n1 // (C * LANES)
    tq, tk = TQ, TK_SOFTMAX
    assert tq == tk
    if kx_shared:
        kx_spec = pl.BlockSpec((1, S, LANES), lambda b, g, i: (b, 0, 0))
    else:
        kx_spec = pl.BlockSpec((1, S, C * LANES), lambda b, g, i: (b, 0, g))
    kern = functools.partial(_attn_softmax_kernel, tq=tq, tk=tk, chunk_shift=chunk_shift, kx_shared=kx_shared)
    per_chain = [pltpu.VMEM((2 * LANES, 2 * tq), bf16), pltpu.VMEM((1, 2 * tq), f32),
                 pltpu.VMEM((HEAD_DIM + BF16_ROWS, tq), f32), pltpu.VMEM((HEAD_DIM + BF16_ROWS, tq), f32),
                 pltpu.VMEM((tk, 2 * tq), f32), pltpu.VMEM((tk, 2 * tq), f32)]
    return pl.pallas_call(
        kern, grid=(B, groups, S // tq),
        in_specs=[pl.BlockSpec((1, C * LANES, tq), lambda b, g, i: (b, g, i)),
                  pl.BlockSpec((1, 2 * C * LANES, tq), lambda b, g, i: (b, g, i)),
                  pl.BlockSpec((1, S, C * LANES), lambda b, g, i: (b, 0, g)),
                  kx_spec,
                  pl.BlockSpec((1, S // tk, C * LANES, tk), lambda b, g, i: (b, 0, g, 0))],
        out_specs=pl.BlockSpec((1, tq, C * LANES), lambda b, g, i: (b, i, g)),
        out_shape=jax.ShapeDtypeStruct((B, S, n1), bf16),
        scratch_shapes=C * per_chain,
        compiler_params=_params("parallel", "parallel", "arbitrary"), name=name,
    )(q1t, qxt, k1, kx, vt)


def _attn_sb_kernel(qt_ref, k_ref, vt_ref, o_ref, *scratch, tq, tk):
    i = pl.program_id(2)
    chains = range(PAIRS_PER_STEP)
    assert tq == 2 * tk
    last = 2 * i + 1
    q2t, r, acc0, acc1, lbx, hlx, r0x, lby, hly, r0y = zip(*_chain_scratch(scratch, 10))
    acc = (acc0, acc1)
    buf_x = (lbx, hlx, r0x)
    buf_y = (lby, hly, r0y)
    for c in chains:
        q2t[c][...] = _pair_queries(qt_ref[0, _tile(c), :])
        r[c][...] = jnp.zeros(r[c].shape, f32)
        for h in range(2):
            acc[h][c][...] = jnp.zeros(acc[h][c].shape, f32)
    tri = (lax.broadcasted_iota(jnp.int32, (tk, tk), 0) < lax.broadcasted_iota(jnp.int32, (tk, tk), 1)).astype(bf16)
    tri2 = jnp.concatenate([tri, tri], axis=1)

    def stage(t_acc, buf_acc, t_log, buf_log, masked):
        if t_acc is not None:
            lb, hl, r0 = buf_acc
            sums = [_dot(tri2, hl[c][...]) for c in chains]
        if t_log is not None:
            ks = pl.multiple_of((last - t_log) * tk, tk)
            z = [_dot(k_ref[0, pl.ds(ks, tk), _tile(c)], q2t[c][...]) for c in chains]
        if t_acc is not None:
            att = []
            for c in chains:
                between = sums[c] + r[c][...]
                r[c][...] = between[0:1, :] + r0[c][...]
                att.append(jnp.exp2(lb[c][...] + between).astype(bf16))
            for c in chains:
                for h in range(2):
                    cols = slice(h * tq, (h + 1) * tq)
                    vt = vt_ref[0, last - t_acc, pl.ds(c * LANES + h * HEAD_DIM, HEAD_DIM), :]
                    acc[h][c][...] += _dot(vt, att[c][:, cols])
        if t_log is not None:
            lb, hl, r0 = buf_log
            if masked:
                kpos, qpos = _positions(z[0].shape, ks, i * tq, tq)
                past = kpos < qpos
            for c in chains:
                log_beta = jnp.minimum(z[c], 0.0) - jnp.log2(1.0 + jnp.exp2(-jnp.abs(z[c])))
                log_rem = log_beta - z[c]
                if masked:
                    log_rem = jnp.where(past, log_rem, 0.0)
                    log_beta = jnp.where(past, log_beta, NEG)
                hi = log_rem.astype(bf16)
                hl[c][:tk] = hi
                hl[c][tk:] = (log_rem - hi.astype(f32)).astype(bf16)
                r0[c][...] = log_rem[0:1, :]
                lb[c][...] = log_beta

    stage(None, None, 0, buf_x, True)
    stage(0, buf_x, 1, buf_y, True)

    def body(u, carry):
        stage(2 * u - 1, buf_y, 2 * u, buf_x, False)
        stage(2 * u, buf_x, 2 * u + 1, buf_y, False)
        return carry

    lax.fori_loop(1, i + 1, body, 0)
    stage(last, buf_y, None, None, False)
    for c in chains:
        o_ref[0, :, _tile(c)] = jnp.concatenate([acc0[c][...], acc1[c][...]], axis=0).T.astype(o_ref.dtype)


def _attn_sb(qt, k, vt):
    B, n1, S = qt.shape
    C = PAIRS_PER_STEP
    groups = n1 // (C * LANES)
    tq, tk = TQ, TK_SB
    kern = functools.partial(_attn_sb_kernel, tq=tq, tk=tk)
    return pl.pallas_call(
        kern, grid=(B, groups, S // tq),
        in_specs=[pl.BlockSpec((1, C * LANES, tq), lambda b, g, i: (b, g, i)),
                  pl.BlockSpec((1, S, C * LANES), lambda b, g, i: (b, 0, g)),
                  pl.BlockSpec((1, S // tk, C * LANES, tk), lambda b, g, i: (b, 0, g, 0))],
        out_specs=pl.BlockSpec((1, tq, C * LANES), lambda b, g, i: (b, i, g)),
        out_shape=jax.ShapeDtypeStruct((B, S, n1), bf16),
        scratch_shapes=C * (
            [pltpu.VMEM((LANES, 2 * tq), bf16), pltpu.VMEM((1, 2 * tq), f32),
             pltpu.VMEM((HEAD_DIM, tq), f32), pltpu.VMEM((HEAD_DIM, tq), f32)]
            + 2 * [pltpu.VMEM((tk, 2 * tq), f32), pltpu.VMEM((2 * tk, 2 * tq), bf16), pltpu.VMEM((1, 2 * tq), f32)]),
        compiler_params=_params("parallel", "parallel", "arbitrary"), name="attn_sb",
    )(qt, k, vt)


def _outproj_kernel(*refs, n_parts):
    o_refs = refs[:n_parts]
    w_refs = refs[n_parts:2 * n_parts]
    x_ref, g_ref, b_ref, y_ref = refs[2 * n_parts:]
    mix = _dot(o_refs[0][...], w_refs[0][...])
    for o_ref, w_ref in zip(o_refs[1:], w_refs[1:]):
        mix = mix + _dot(o_ref[...], w_ref[...])
    y_ref[...] = _layer_norm(DEEPNORM_ALPHA * x_ref[...] + mix, g_ref[...], b_ref[...])


def _outproj(parts, weights, x, g, b):
    M, D = x.shape
    tm = TOK_TILE
    n = len(parts)
    in_specs = [pl.BlockSpec((tm, o.shape[1]), lambda i: (i, 0)) for o in parts]
    in_specs += [_const_spec(w.shape) for w in weights]
    in_specs += [pl.BlockSpec((tm, D), lambda i: (i, 0)), _const_spec(g.shape), _const_spec(b.shape)]
    return pl.pallas_call(
        functools.partial(_outproj_kernel, n_parts=n), grid=(M // tm,),
        in_specs=in_specs, out_specs=pl.BlockSpec((tm, D), lambda i: (i, 0)),
        out_shape=jax.ShapeDtypeStruct((M, D), f32),
        compiler_params=_params("parallel"), name="outproj_ln",
    )(*parts, *weights, x, g, b)


def _ffn_kernel(x_ref, p_ref, w1_ref, w3_ref, w2_ref, g_ref, b_ref, wg_ref, bg_ref, wp_ref, y_ref):
    x = x_ref[...]
    xb = x.astype(bf16)
    h1 = _dot(xb, w1_ref[...])
    h3 = _dot(xb, w3_ref[...])
    act = (h1 * jax.nn.sigmoid(h1) * h3).astype(bf16)
    x2 = _layer_norm(DEEPNORM_ALPHA * x + _dot(act, w2_ref[...]), g_ref[...], b_ref[...])
    gate = jax.nn.sigmoid(_dot(x2.astype(bf16), wg_ref[...]) + bg_ref[...])
    y_ref[...] = x2 + gate * _dot(p_ref[...].astype(bf16), wp_ref[...])


def _ffn(x, p, w1, w3, w2, g, b, wg, bg, wp):
    M, D = x.shape
    tm = TOK_TILE
    consts = [w1, w3, w2, g, b, wg, bg, wp]
    in_specs = [pl.BlockSpec((tm, D), lambda i: (i, 0)), pl.BlockSpec((tm, p.shape[1]), lambda i: (i, 0))]
    in_specs += [_const_spec(c.shape) for c in consts]
    return pl.pallas_call(
        _ffn_kernel, grid=(M // tm,), in_specs=in_specs,
        out_specs=pl.BlockSpec((tm, D), lambda i: (i, 0)),
        out_shape=jax.ShapeDtypeStruct((M, D), f32),
        compiler_params=_params("parallel"), name="ffn_ln_ple",
    )(x, p, *consts)


def _proj_c_kernel(x_ref, wq_ref, wk_ref, wv_ref, wf_ref, bf_ref, qt_ref, k_ref, vt_ref, lf_ref):
    xb = x_ref[0].astype(bf16)
    qt_ref[0] = (_dot_nt(wq_ref[...], xb) * HEAD_SCALE).astype(bf16)
    k_ref[0] = _dot(xb, wk_ref[...]).astype(bf16)
    vt = _dot_nt(wv_ref[...], xb).astype(bf16)
    tm = vt.shape[1]
    for c in range(tm // TK_SOFTMAX):
        vt_ref[0, c] = vt[:, c * TK_SOFTMAX:(c + 1) * TK_SOFTMAX]
    lf_ref[0] = _log_sigmoid(_dot(xb, wf_ref[...]) + bf_ref[...])


def _proj_c(x, w):
    B, S, D = x.shape
    tm = TOK_TILE
    n = FOX_HEADS * HEAD_DIM
    consts = [w["q_t"], w["k"], w["v_t"], w["f"], w["bf"]]
    in_specs = [pl.BlockSpec((1, tm, D), lambda b, i: (b, i, 0))] + [_const_spec(c.shape) for c in consts]
    out_shape = [jax.ShapeDtypeStruct((B, n, S), bf16),
                 jax.ShapeDtypeStruct((B, S, n), bf16),
                 jax.ShapeDtypeStruct((B, S // TK_SOFTMAX, n, TK_SOFTMAX), bf16),
                 jax.ShapeDtypeStruct((B, S, LANES), f32)]
    out_specs = [pl.BlockSpec((1, n, tm), lambda b, i: (b, 0, i)),
                 pl.BlockSpec((1, tm, n), lambda b, i: (b, i, 0)),
                 pl.BlockSpec((1, tm // TK_SOFTMAX, n, TK_SOFTMAX), lambda b, i: (b, i, 0, 0)),
                 pl.BlockSpec((1, tm, LANES), lambda b, i: (b, i, 0))]
    return pl.pallas_call(
        _proj_c_kernel, grid=(B, S // tm), in_specs=in_specs, out_specs=out_specs, out_shape=out_shape,
        compiler_params=_params("parallel", "parallel"), name="proj_c",
    )(x, *consts)


CUM_BLOCK = 256


def _split3(v):
    hi = v.astype(bf16)
    r = v - hi.astype(f32)
    mid = r.astype(bf16)
    lo = (r - mid.astype(f32)).astype(bf16)
    return hi, mid, lo


def _fox_bias_kernel(lf_ref, selq_ref, selk_ref, qaug_ref, kaug_ref, carry_ref):
    n = CUM_BLOCK

    @pl.when(pl.program_id(1) == 0)
    def _():
        carry_ref[...] = jnp.zeros(carry_ref.shape, f32)

    lower = (lax.broadcasted_iota(jnp.int32, (n, n), 0) >= lax.broadcasted_iota(jnp.int32, (n, n), 1)).astype(bf16)
    lane = lax.broadcasted_iota(jnp.int32, (n, LANES), 1)
    hi, mid, lo = _split3(lf_ref[0])
    d = _dot(lower, hi) + _dot(lower, mid) + _dot(lower, lo) + carry_ref[...]
    carry_ref[...] = d[n - 1:n, :]
    pieces = [jnp.where(lane == LANES - 1, 1.0, v.astype(f32)).astype(bf16) for v in _split3(d * LOG2E)]
    qaug = _dot_nt(selq_ref[0], pieces[0]) + _dot_nt(selq_ref[1], pieces[1]) + _dot_nt(selq_ref[2], pieces[2])
    kaug = _dot(pieces[0], selk_ref[0]) + _dot(pieces[1], selk_ref[1]) + _dot(pieces[2], selk_ref[2])
    qaug_ref[0] = qaug.astype(bf16)
    kaug_ref[0] = kaug.astype(bf16)


def _fox_selectors():
    selq = np.zeros((3, FOX_HEADS * LANES, LANES), np.float32)
    selk = np.zeros((3, LANES, (FOX_HEADS // 2) * LANES), np.float32)
    one = LANES - 1
    for h in range(FOX_HEADS):
        pair, e = divmod(h, 2)
        for c in range(3):
            selq[c, h * LANES + 6 * e + c, h] = 1.0
            selk[c, h, pair * LANES + 6 * e + 3 + c] = -1.0
        for c in range(3):
            selq[0, h * LANES + 6 * e + 3 + c, one] = 1.0
            selk[0, one, pair * LANES + 6 * e + c] = 1.0
    return jnp.asarray(selq, bf16), jnp.asarray(selk, bf16)


def _fox_bias(lf):
    B, S, _ = lf.shape
    selq, selk = _fox_selectors()
    n = CUM_BLOCK
    nk = (FOX_HEADS // 2) * LANES
    return pl.pallas_call(
        _fox_bias_kernel, grid=(B, S // n),
        in_specs=[pl.BlockSpec((1, n, LANES), lambda b, j: (b, j, 0)), _const_spec(selq.shape), _const_spec(selk.shape)],
        out_specs=[pl.BlockSpec((1, FOX_HEADS * LANES, n), lambda b, j: (b, 0, j)),
                   pl.BlockSpec((1, n, nk), lambda b, j: (b, j, 0))],
        out_shape=[jax.ShapeDtypeStruct((B, FOX_HEADS * LANES, S), bf16),
                   jax.ShapeDtypeStruct((B, S, nk), bf16)],
        scratch_shapes=[pltpu.VMEM((1, LANES), f32)],
        compiler_params=_params("parallel", "arbitrary"), name="fox_bias",
    )(lf, selq, selk)


def _rotate_half_cols(w):
    half = w.shape[-1] // 2
    return jnp.concatenate([-w[..., half:], w[..., :half]], axis=-1)


def _pad_lanes(w):
    pad = [(0, 0)] * (w.ndim - 1) + [(0, LANES - w.shape[-1])]
    return jnp.pad(w, pad)


def _prep_layer_a(w_in, q_norm, w_uq, kv_norm, w_ukv):
    in_a = Q_LORA + KV_LORA + MLA_ROPE
    n = SB_HEADS * HEAD_DIM
    kr = w_in[:, Q_LORA + KV_LORA:in_a]
    sb = w_in[:, in_a:]
    uq = w_uq.reshape(Q_LORA, MLA_HEADS, MLA_NOPE + MLA_ROPE)
    rope = uq[:, :, MLA_NOPE:]
    ukv = w_ukv.reshape(KV_LORA, MLA_HEADS, MLA_NOPE + HEAD_DIM)
    return {
        "lat": w_in[:, :Q_LORA + KV_LORA].astype(bf16),
        "kr": jnp.concatenate([_pad_lanes(kr), _pad_lanes(_rotate_half_cols(kr))], axis=1).astype(bf16),
        "sbq_t": sb[:, :n].T.astype(bf16),
        "sbk": sb[:, n:2 * n].astype(bf16),
        "sbv_t": sb[:, 2 * n:].T.astype(bf16),
        "gq": q_norm.reshape(1, Q_LORA),
        "gkv": kv_norm.reshape(1, KV_LORA),
        "qn_t": uq[:, :, :MLA_NOPE].reshape(Q_LORA, -1).T.astype(bf16),
        "qra_t": _pad_lanes(rope).reshape(Q_LORA, -1).T.astype(bf16),
        "qrb_t": _pad_lanes(_rotate_half_cols(rope)).reshape(Q_LORA, -1).T.astype(bf16),
        "kn": ukv[:, :, :MLA_NOPE].reshape(KV_LORA, -1).astype(bf16),
        "v_t": ukv[:, :, MLA_NOPE:].reshape(KV_LORA, -1).T.astype(bf16),
    }


def _prep_layer_c(w_in, b_f):
    n = FOX_HEADS * HEAD_DIM
    return {
        "q_t": w_in[:, :n].T.astype(bf16),
        "k": w_in[:, n:2 * n].astype(bf16),
        "v_t": w_in[:, 2 * n:3 * n].T.astype(bf16),
        "f": _pad_lanes(w_in[:, 3 * n:]).astype(bf16),
        "bf": _pad_lanes(b_f.reshape(1, FOX_HEADS)),
    }


def _rope_tables(seq):
    inv = 1.0 / (ROPE_BASE ** (jnp.arange(0, MLA_ROPE, 2, dtype=f32) / MLA_ROPE))
    ang = jnp.arange(seq, dtype=f32)[:, None] * inv[None, :]
    cos = _pad_lanes(jnp.concatenate([jnp.cos(ang), jnp.cos(ang)], axis=1))
    sin = _pad_lanes(jnp.concatenate([jnp.sin(ang), jnp.sin(ang)], axis=1))
    return cos, sin, cos.T, sin.T


def kernel(x, p, a_w_in, a_q_norm, a_w_uq, a_kv_norm, a_w_ukv, a_w_out, c_w_in, c_b_f, c_w_out,
           ffn_w1, ffn_w3, ffn_w2, ln1_g, ln1_b, ln2_g, ln2_b, ple_w_proj, ple_w_gate, ple_b_gate):
    B, S, D = x.shape
    M = B * S
    row = lambda v: v.reshape(1, -1)

    def channel_mixer(x1, i):
        return _ffn(x1, p[i].reshape(M, P_DIM), ffn_w1[i].astype(bf16), ffn_w3[i].astype(bf16),
                    ffn_w2[i].astype(bf16), row(ln2_g[i]), row(ln2_b[i]), ple_w_gate[i].astype(bf16),
                    row(ple_b_gate[i]), ple_w_proj[i].astype(bf16))

    wa = _prep_layer_a(a_w_in[0], a_q_norm[0], a_w_uq[0], a_kv_norm[0], a_w_ukv[0])
    cos, sin, cost, sint = _rope_tables(S)
    qnt, qrt, kn, kr, vat, qbt, kb, vbt = _proj_a(x, wa, cos, sin, cost, sint)
    o_a = _attn_softmax(qnt, qrt, kn, kr, vat, kx_shared=True, chunk_shift=SEQ_CHUNK.bit_length() - 1, name="attn_mla")
    o_b = _attn_sb(qbt, kb, vbt)
    na = MLA_HEADS * HEAD_DIM
    w_out = a_w_out[0].astype(bf16)
    x1 = _outproj([o_a.reshape(M, -1), o_b.reshape(M, -1)], [w_out[:na], w_out[na:]], x.reshape(M, D),
                  row(ln1_g[0]), row(ln1_b[0]))
    x2 = channel_mixer(x1, 0)

    wc = _prep_layer_c(c_w_in[0], c_b_f[0])
    qt, k, vt, lf = _proj_c(x2.reshape(B, S, D), wc)
    qaug, kaug = _fox_bias(lf)
    o_c = _attn_softmax(qt, qaug, k, kaug, vt, kx_shared=False, chunk_shift=0, name="attn_fox")
    x3 = _outproj([o_c.reshape(M, -1)], [c_w_out[0].astype(bf16)], x2, row(ln1_g[1]), row(ln1_b[1]))
    x4 = channel_mixer(x3, 1)
    return x4.reshape(B, S, D)
```

```python
import functools

import jax
import jax.numpy as jnp
import numpy as np
from jax import lax
from jax.experimental import pallas as pl
from jax.experimental.pallas import tpu as pltpu

D_MODEL = 1024
SEQ_CHUNK = 64
P_DIM = 256
MLA_HEADS = 8
MLA_NOPE = 64
MLA_ROPE = 32
Q_LORA = 256
KV_LORA = 256
ROPE_BASE = 10000.0
SB_HEADS = 8
FOX_HEADS = 16
HEAD_DIM = 64
DEPTH = 2
DEEPNORM_ALPHA = (2.0 * DEPTH) ** 0.25
LOG2E = 1.4426950408889634
MLA_SCALE = (MLA_NOPE + MLA_ROPE) ** -0.5 * LOG2E
HEAD_SCALE = HEAD_DIM ** -0.5 * LOG2E

LANES = 128
BF16_ROWS = 16
VMEM_LIMIT = 56 * 1024 * 1024

TOK_TILE = 512
TQ = 256
PAIRS_PER_STEP = 4
TK_SOFTMAX = 256
TK_SB = 128
NEG = -1e30
AUG_ROWS = 16

bf16 = jnp.bfloat16
f32 = jnp.float32


def _dot(a, b):
    return jnp.dot(a, b, preferred_element_type=f32)


def _dot_nt(a, b):
    return lax.dot_general(a, b, (((1,), (1,)), ((), ())), preferred_element_type=f32)


def _layer_norm(z, g, b):
    mu = jnp.mean(z, axis=-1, keepdims=True)
    zc = z - mu
    var = jnp.mean(zc * zc, axis=-1, keepdims=True)
    return zc * lax.rsqrt(var + 1e-5) * g + b


def _rms_norm(c, g):
    return c * lax.rsqrt(jnp.mean(c * c, axis=-1, keepdims=True) + 1e-6) * g


def _log_sigmoid(a):
    return jnp.minimum(a, 0.0) - jnp.log(1.0 + jnp.exp(-jnp.abs(a)))


def _const_spec(shape):
    nd = len(shape)
    return pl.BlockSpec(shape, lambda *_: (0,) * nd, pipeline_mode=pl.Buffered(1))


def _params(*sem):
    return pltpu.CompilerParams(dimension_semantics=sem, vmem_limit_bytes=VMEM_LIMIT)


def _proj_a_kernel(x_ref, wlat_ref, wkr_ref, wsbq_ref, wsbk_ref, wsbv_ref, gq_ref, gkv_ref,
                   wqn_ref, wqra_ref, wqrb_ref, wkn_ref, wv_ref, cos_ref, sin_ref, cost_ref, sint_ref,
                   qnt_ref, qrt_ref, kn_ref, kr_ref, vat_ref, qbt_ref, kb_ref, vbt_ref):
    xb = x_ref[0].astype(bf16)
    lat = _dot(xb, wlat_ref[...])
    cq = _rms_norm(lat[:, :Q_LORA], gq_ref[...]).astype(bf16)
    ckv = _rms_norm(lat[:, Q_LORA:], gkv_ref[...]).astype(bf16)

    qnt_ref[0] = (_dot_nt(wqn_ref[...], cq) * MLA_SCALE).astype(bf16)
    qa = _dot_nt(wqra_ref[...], cq)
    qb = _dot_nt(wqrb_ref[...], cq)
    cost = cost_ref[...]
    sint = sint_ref[...]
    for h in range(MLA_HEADS):
        sl = slice(h * MLA_ROPE, (h + 1) * MLA_ROPE)
        qrt_ref[0, sl, :] = ((qa[sl] * cost + qb[sl] * sint) * MLA_SCALE).astype(bf16)

    kn_ref[0] = _dot(ckv, wkn_ref[...]).astype(bf16)
    kr2 = _dot(xb, wkr_ref[...])
    kr_ref[0] = (kr2[:, :LANES] * cos_ref[...] + kr2[:, LANES:] * sin_ref[...]).astype(bf16)

    vat = _dot_nt(wv_ref[...], ckv).astype(bf16)
    tm = vat.shape[1]
    for c in range(tm // TK_SOFTMAX):
        vat_ref[0, c] = vat[:, c * TK_SOFTMAX:(c + 1) * TK_SOFTMAX]

    qbt_ref[0] = (_dot_nt(wsbq_ref[...], xb) * HEAD_SCALE).astype(bf16)
    kb_ref[0] = _dot(xb, wsbk_ref[...]).astype(bf16)
    vbt = _dot_nt(wsbv_ref[...], xb).astype(bf16)
    for c in range(tm // TK_SB):
        vbt_ref[0, c] = vbt[:, c * TK_SB:(c + 1) * TK_SB]


def _proj_a(x, w, cos, sin, cost, sint):
    B, S, D = x.shape
    tm = TOK_TILE
    na = MLA_HEADS * HEAD_DIM
    nb = SB_HEADS * HEAD_DIM
    consts = [w["lat"], w["kr"], w["sbq_t"], w["sbk"], w["sbv_t"], w["gq"], w["gkv"],
              w["qn_t"], w["qra_t"], w["qrb_t"], w["kn"], w["v_t"]]
    in_specs = [pl.BlockSpec((1, tm, D), lambda b, i: (b, i, 0))]
    in_specs += [_const_spec(c.shape) for c in consts]
    in_specs += [pl.BlockSpec((tm, LANES), lambda b, i: (i, 0)),
                 pl.BlockSpec((tm, LANES), lambda b, i: (i, 0)),
                 pl.BlockSpec((MLA_ROPE, tm), lambda b, i: (0, i)),
                 pl.BlockSpec((MLA_ROPE, tm), lambda b, i: (0, i))]
    out_shape = [
        jax.ShapeDtypeStruct((B, na, S), bf16),
        jax.ShapeDtypeStruct((B, MLA_HEADS * MLA_ROPE, S), bf16),
        jax.ShapeDtypeStruct((B, S, na), bf16),
        jax.ShapeDtypeStruct((B, S, LANES), bf16),
        jax.ShapeDtypeStruct((B, S // TK_SOFTMAX, na, TK_SOFTMAX), bf16),
        jax.ShapeDtypeStruct((B, nb, S), bf16),
        jax.ShapeDtypeStruct((B, S, nb), bf16),
        jax.ShapeDtypeStruct((B, S // TK_SB, nb, TK_SB), bf16),
    ]
    out_specs = [
        pl.BlockSpec((1, na, tm), lambda b, i: (b, 0, i)),
        pl.BlockSpec((1, MLA_HEADS * MLA_ROPE, tm), lambda b, i: (b, 0, i)),
        pl.BlockSpec((1, tm, na), lambda b, i: (b, i, 0)),
        pl.BlockSpec((1, tm, LANES), lambda b, i: (b, i, 0)),
        pl.BlockSpec((1, tm // TK_SOFTMAX, na, TK_SOFTMAX), lambda b, i: (b, i, 0, 0)),
        pl.BlockSpec((1, nb, tm), lambda b, i: (b, 0, i)),
        pl.BlockSpec((1, tm, nb), lambda b, i: (b, i, 0)),
        pl.BlockSpec((1, tm // TK_SB, nb, TK_SB), lambda b, i: (b, i, 0, 0)),
    ]
    return pl.pallas_call(
        _proj_a_kernel, grid=(B, S // tm), in_specs=in_specs, out_specs=out_specs, out_shape=out_shape,
        compiler_params=_params("parallel", "parallel"), name="proj_a",
    )(x, *consts, cos, sin, cost, sint)


def _pair_queries(q1):
    row = lax.broadcasted_iota(jnp.int32, q1.shape, 0)
    zero = jnp.zeros_like(q1)
    return jnp.concatenate([jnp.where(row < HEAD_DIM, q1, zero), jnp.where(row >= HEAD_DIM, q1, zero)], axis=1)


def _positions(shape, key_start, query_start, tq):
    kpos = key_start + lax.broadcasted_iota(jnp.int32, shape, 0)
    col = lax.broadcasted_iota(jnp.int32, shape, 1)
    qpos = query_start + jnp.where(col >= tq, col - tq, col)
    return kpos, qpos


def _tile(c):
    return slice(c * LANES, (c + 1) * LANES)


def _chain_scratch(refs, per_chain):
    return [refs[c * per_chain:(c + 1) * per_chain] for c in range(PAIRS_PER_STEP)]


def _attn_softmax_kernel(q1_ref, qx_ref, k1_ref, kx_ref, vt_ref, o_ref, *scratch, tq, tk, chunk_shift, kx_shared):
    i = pl.program_id(2)
    chains = range(PAIRS_PER_STEP)
    xr = qx_ref.shape[1] // (2 * PAIRS_PER_STEP)
    q2t, m, acc0, acc1, sa, sb = zip(*_chain_scratch(scratch, 6))
    acc = (acc0, acc1)
    for c in chains:
        qx = jnp.concatenate([qx_ref[0, pl.ds(2 * c * xr, xr), :], qx_ref[0, pl.ds((2 * c + 1) * xr, xr), :]], axis=1)
        pad = jnp.zeros((LANES - xr, 2 * tq), bf16)
        q2t[c][...] = jnp.concatenate([_pair_queries(q1_ref[0, _tile(c), :]), qx, pad], axis=0)
        m[c][...] = jnp.full(m[c].shape, NEG, f32)
        for h in range(2):
            acc[h][c][...] = jnp.zeros(acc[h][c].shape, f32)
    ones = jnp.ones((BF16_ROWS, tk), bf16)

    def scores(j, s_refs):
        ks = pl.multiple_of(j * tk, tk)
        for c in chains:
            kx = kx_ref[0, pl.ds(ks, tk), _tile(0 if kx_shared else c)]
            kb = jnp.concatenate([k1_ref[0, pl.ds(ks, tk), _tile(c)], kx], axis=1)
            s_refs[c][...] = _dot(kb, q2t[c][...])

    def accumulate(j, s_refs, masked, j_next=None, next_refs=None):
        s = [s_refs[c][...] for c in chains]
        if masked:
            kpos, qpos = _positions(s[0].shape, j * tk, i * tq, tq)
            ok = (kpos >> chunk_shift) <= (qpos >> chunk_shift)
            s = [jnp.where(ok, sc, NEG) for sc in s]
        m_new, alpha = [], []
        for c in chains:
            m_old = m[c][...]
            m_new.append(jnp.maximum(m_old, jnp.max(s[c], axis=0, keepdims=True)))
            alpha.append(jnp.exp2(m_old - m_new[c]))
            m[c][...] = m_new[c]
        if next_refs is not None:
            scores(j_next, next_refs)
        for c in chains:
            p = jnp.exp2(s[c] - m_new[c]).astype(bf16)
            for h in range(2):
                vta = jnp.concatenate([vt_ref[0, j, pl.ds(c * LANES + h * HEAD_DIM, HEAD_DIM), :], ones], axis=0)
                cols = slice(h * tq, (h + 1) * tq)
                acc[h][c][...] = alpha[c][:, cols] * acc[h][c][...] + _dot(vta, p[:, cols])

    scores(0, sa)

    def body(u, carry):
        accumulate(2 * u, sa, False, 2 * u + 1, sb)
        accumulate(2 * u + 1, sb, False, 2 * u + 2, sa)
        return carry

    lax.fori_loop(0, i // 2, body, 0)

    @pl.when(i % 2 == 0)
    def _():
        accumulate(i, sa, True)

    @pl.when(i % 2 == 1)
    def _():
        accumulate(i - 1, sa, False, i, sb)
        accumulate(i, sb, True)

    for c in chains:
        heads = []
        for h in range(2):
            a = acc[h][c][...]
            heads.append(a[:HEAD_DIM] * (1.0 / a[HEAD_DIM:HEAD_DIM + 1]))
        o_ref[0, :, _tile(c)] = jnp.concatenate(heads, axis=0).T.astype(o_ref.dtype)


def _attn_softmax(q1t, qxt, k1, kx, vt, *, kx_shared, chunk_shift, name):
    B, n1, S = q1t.shape
    C = PAIRS_PER_STEP
    groups = n1 // (C * LANES)
    tq, tk = TQ, TK_SOFTMAX
    assert tq == tk
    xr = qxt.shape[1] * LANES // (2 * n1)
    if kx_shared:
        kx_spec = pl.BlockSpec((1, S, LANES), lambda b, g, i: (b, 0, 0))
    else:
        kx_spec = pl.BlockSpec((1, S, C * LANES), lambda b, g, i: (b, 0, g))
    kern = functools.partial(_attn_softmax_kernel, tq=tq, tk=tk, chunk_shift=chunk_shift, kx_shared=kx_shared)
    per_chain = [pltpu.VMEM((2 * LANES, 2 * tq), bf16), pltpu.VMEM((1, 2 * tq), f32),
                 pltpu.VMEM((HEAD_DIM + BF16_ROWS, tq), f32), pltpu.VMEM((HEAD_DIM + BF16_ROWS, tq), f32),
                 pltpu.VMEM((tk, 2 * tq), f32), pltpu.VMEM((tk, 2 * tq), f32)]
    return pl.pallas_call(
        kern, grid=(B, groups, S // tq),
        in_specs=[pl.BlockSpec((1, C * LANES, tq), lambda b, g, i: (b, g, i)),
                  pl.BlockSpec((1, 2 * C * xr, tq), lambda b, g, i: (b, g, i)),
                  pl.BlockSpec((1, S, C * LANES), lambda b, g, i: (b, 0, g)),
                  kx_spec,
                  pl.BlockSpec((1, S // tk, C * LANES, tk), lambda b, g, i: (b, 0, g, 0))],
        out_specs=pl.BlockSpec((1, tq, C * LANES), lambda b, g, i: (b, i, g)),
        out_shape=jax.ShapeDtypeStruct((B, S, n1), bf16),
        scratch_shapes=C * per_chain,
        compiler_params=_params("parallel", "parallel", "arbitrary"), name=name,
    )(q1t, qxt, k1, kx, vt)


def _attn_sb_kernel(qt_ref, k_ref, vt_ref, o_ref, *scratch, tq, tk):
    i = pl.program_id(2)
    chains = range(PAIRS_PER_STEP)
    assert tq == 2 * tk
    last = 2 * i + 1
    q2t, r, acc0, acc1, z0, z1, lb0, lr0, f0, lb1, lr1, f1 = zip(*_chain_scratch(scratch, 12))
    acc = (acc0, acc1)
    zbuf = (z0, z1)
    sbuf = ((lb0, lr0, f0), (lb1, lr1, f1))
    for c in chains:
        q2t[c][...] = _pair_queries(qt_ref[0, _tile(c), :])
        r[c][...] = jnp.zeros(r[c].shape, f32)
        for h in range(2):
            acc[h][c][...] = jnp.zeros(acc[h][c].shape, f32)
    tri = (lax.broadcasted_iota(jnp.int32, (tk, tk), 0) < lax.broadcasted_iota(jnp.int32, (tk, tk), 1)).astype(bf16)

    def half(weights=None, logits=None, stats=None, masked=False):
        if weights is not None:
            t_w, par = weights
            lb, lr, first = sbuf[par]
            sums = [_dot(tri, lr[c][...]) for c in chains]
        if logits is not None:
            t_z, par = logits
            ks = pl.multiple_of((last - t_z) * tk, tk)
            for c in chains:
                zbuf[par][c][...] = _dot(k_ref[0, pl.ds(ks, tk), _tile(c)], q2t[c][...])
        if stats is not None:
            t_s, par = stats
            lb_s, lr_s, first_s = sbuf[par]
            if masked:
                kpos, qpos = _positions((tk, 2 * tq), (last - t_s) * tk, i * tq, tq)
                past = kpos < qpos
            for c in chains:
                z = zbuf[par][c][...]
                log_beta = jnp.minimum(z, 0.0) - jnp.log2(1.0 + jnp.exp2(-jnp.abs(z)))
                log_rem = log_beta - z
                if masked:
                    log_rem = jnp.where(past, log_rem, 0.0)
                    log_beta = jnp.where(past, log_beta, NEG)
                lr_s[c][...] = log_rem.astype(bf16)
                first_s[c][...] = log_rem[0:1, :]
                lb_s[c][...] = log_beta
        if weights is not None:
            att = []
            for c in chains:
                between = sums[c] + r[c][...]
                r[c][...] = between[0:1, :] + first[c][...]
                att.append(jnp.exp2(lb[c][...] + between).astype(bf16))
            for c in chains:
                for h in range(2):
                    cols = slice(h * tq, (h + 1) * tq)
                    vt = vt_ref[0, last - t_w, pl.ds(c * LANES + h * HEAD_DIM, HEAD_DIM), :]
                    acc[h][c][...] += _dot(vt, att[c][:, cols])

    half(logits=(0, 0))
    half(logits=(1, 1), stats=(0, 0), masked=True)
    half(weights=(0, 0), logits=(jnp.minimum(2, last), 0), stats=(1, 1), masked=True)

    def body(u, carry):
        half(weights=(2 * u - 1, 1), logits=(2 * u + 1, 1), stats=(2 * u, 0))
        half(weights=(2 * u, 0), logits=(2 * u + 2, 0), stats=(2 * u + 1, 1))
        return carry

    lax.fori_loop(1, i, body, 0)

    @pl.when(i >= 1)
    def _():
        half(weights=(2 * i - 1, 1), logits=(last, 1), stats=(2 * i, 0))
        half(weights=(2 * i, 0), stats=(last, 1))

    half(weights=(last, 1))
    for c in chains:
        o_ref[0, :, _tile(c)] = jnp.concatenate([acc0[c][...], acc1[c][...]], axis=0).T.astype(o_ref.dtype)


def _attn_sb(qt, k, vt):
    B, n1, S = qt.shape
    C = PAIRS_PER_STEP
    groups = n1 // (C * LANES)
    tq, tk = TQ, TK_SB
    kern = functools.partial(_attn_sb_kernel, tq=tq, tk=tk)
    return pl.pallas_call(
        kern, grid=(B, groups, S // tq),
        in_specs=[pl.BlockSpec((1, C * LANES, tq), lambda b, g, i: (b, g, i)),
                  pl.BlockSpec((1, S, C * LANES), lambda b, g, i: (b, 0, g)),
                  pl.BlockSpec((1, S // tk, C * LANES, tk), lambda b, g, i: (b, 0, g, 0))],
        out_specs=pl.BlockSpec((1, tq, C * LANES), lambda b, g, i: (b, i, g)),
        out_shape=jax.ShapeDtypeStruct((B, S, n1), bf16),
        scratch_shapes=C * (
            [pltpu.VMEM((LANES, 2 * tq), bf16), pltpu.VMEM((1, 2 * tq), f32),
             pltpu.VMEM((HEAD_DIM, tq), f32), pltpu.VMEM((HEAD_DIM, tq), f32)]
            + 2 * [pltpu.VMEM((tk, 2 * tq), f32)]
            + 2 * [pltpu.VMEM((tk, 2 * tq), f32), pltpu.VMEM((tk, 2 * tq), bf16), pltpu.VMEM((1, 2 * tq), f32)]),
        compiler_params=_params("parallel", "parallel", "arbitrary"), name="attn_sb",
    )(qt, k, vt)


def _outproj_kernel(*refs, n_parts):
    o_refs = refs[:n_parts]
    w_refs = refs[n_parts:2 * n_parts]
    x_ref, g_ref, b_ref, y_ref = refs[2 * n_parts:]
    mix = _dot(o_refs[0][...], w_refs[0][...])
    for o_ref, w_ref in zip(o_refs[1:], w_refs[1:]):
        mix = mix + _dot(o_ref[...], w_ref[...])
    y_ref[...] = _layer_norm(DEEPNORM_ALPHA * x_ref[...] + mix, g_ref[...], b_ref[...])


def _outproj(parts, weights, x, g, b):
    M, D = x.shape
    tm = TOK_TILE
    n = len(parts)
    in_specs = [pl.BlockSpec((tm, o.shape[1]), lambda i: (i, 0)) for o in parts]
    in_specs += [_const_spec(w.shape) for w in weights]
    in_specs += [pl.BlockSpec((tm, D), lambda i: (i, 0)), _const_spec(g.shape), _const_spec(b.shape)]
    return pl.pallas_call(
        functools.partial(_outproj_kernel, n_parts=n), grid=(M // tm,),
        in_specs=in_specs, out_specs=pl.BlockSpec((tm, D), lambda i: (i, 0)),
        out_shape=jax.ShapeDtypeStruct((M, D), f32),
        compiler_params=_params("parallel"), name="outproj_ln",
    )(*parts, *weights, x, g, b)


def _ffn_kernel(x_ref, p_ref, w1_ref, w3_ref, w2_ref, g_ref, b_ref, wg_ref, bg_ref, wp_ref, y_ref):
    x = x_ref[...]
    xb = x.astype(bf16)
    h1 = _dot(xb, w1_ref[...])
    h3 = _dot(xb, w3_ref[...])
    act = (h1 * jax.nn.sigmoid(h1) * h3).astype(bf16)
    x2 = _layer_norm(DEEPNORM_ALPHA * x + _dot(act, w2_ref[...]), g_ref[...], b_ref[...])
    gate = jax.nn.sigmoid(_dot(x2.astype(bf16), wg_ref[...]) + bg_ref[...])
    y_ref[...] = x2 + gate * _dot(p_ref[...].astype(bf16), wp_ref[...])


def _ffn(x, p, w1, w3, w2, g, b, wg, bg, wp):
    M, D = x.shape
    tm = TOK_TILE
    consts = [w1, w3, w2, g, b, wg, bg, wp]
    in_specs = [pl.BlockSpec((tm, D), lambda i: (i, 0)), pl.BlockSpec((tm, p.shape[1]), lambda i: (i, 0))]
    in_specs += [_const_spec(c.shape) for c in consts]
    return pl.pallas_call(
        _ffn_kernel, grid=(M // tm,), in_specs=in_specs,
        out_specs=pl.BlockSpec((tm, D), lambda i: (i, 0)),
        out_shape=jax.ShapeDtypeStruct((M, D), f32),
        compiler_params=_params("parallel"), name="ffn_ln_ple",
    )(x, p, *consts)


def _proj_c_kernel(x_ref, wq_ref, wk_ref, wv_ref, wf_ref, bf_ref, qt_ref, k_ref, vt_ref, lf_ref):
    xb = x_ref[0].astype(bf16)
    qt_ref[0] = (_dot_nt(wq_ref[...], xb) * HEAD_SCALE).astype(bf16)
    k_ref[0] = _dot(xb, wk_ref[...]).astype(bf16)
    vt = _dot_nt(wv_ref[...], xb).astype(bf16)
    tm = vt.shape[1]
    for c in range(tm // TK_SOFTMAX):
        vt_ref[0, c] = vt[:, c * TK_SOFTMAX:(c + 1) * TK_SOFTMAX]
    lf_ref[0] = _log_sigmoid(_dot(xb, wf_ref[...]) + bf_ref[...])


def _proj_c(x, w):
    B, S, D = x.shape
    tm = TOK_TILE
    n = FOX_HEADS * HEAD_DIM
    consts = [w["q_t"], w["k"], w["v_t"], w["f"], w["bf"]]
    in_specs = [pl.BlockSpec((1, tm, D), lambda b, i: (b, i, 0))] + [_const_spec(c.shape) for c in consts]
    out_shape = [jax.ShapeDtypeStruct((B, n, S), bf16),
                 jax.ShapeDtypeStruct((B, S, n), bf16),
                 jax.ShapeDtypeStruct((B, S // TK_SOFTMAX, n, TK_SOFTMAX), bf16),
                 jax.ShapeDtypeStruct((B, S, LANES), f32)]
    out_specs = [pl.BlockSpec((1, n, tm), lambda b, i: (b, 0, i)),
                 pl.BlockSpec((1, tm, n), lambda b, i: (b, i, 0)),
                 pl.BlockSpec((1, tm // TK_SOFTMAX, n, TK_SOFTMAX), lambda b, i: (b, i, 0, 0)),
                 pl.BlockSpec((1, tm, LANES), lambda b, i: (b, i, 0))]
    return pl.pallas_call(
        _proj_c_kernel, grid=(B, S // tm), in_specs=in_specs, out_specs=out_specs, out_shape=out_shape,
        compiler_params=_params("parallel", "parallel"), name="proj_c",
    )(x, *consts)


CUM_BLOCK = 256


def _split3(v):
    hi = v.astype(bf16)
    r = v - hi.astype(f32)
    mid = r.astype(bf16)
    lo = (r - mid.astype(f32)).astype(bf16)
    return hi, mid, lo


def _fox_bias_kernel(lf_ref, selq_ref, selk_ref, qaug_ref, kaug_ref, carry_ref):
    n = CUM_BLOCK

    @pl.when(pl.program_id(1) == 0)
    def _():
        carry_ref[...] = jnp.zeros(carry_ref.shape, f32)

    lower = (lax.broadcasted_iota(jnp.int32, (n, n), 0) >= lax.broadcasted_iota(jnp.int32, (n, n), 1)).astype(bf16)
    lane = lax.broadcasted_iota(jnp.int32, (n, LANES), 1)
    hi, mid, lo = _split3(lf_ref[0])
    d = _dot(lower, hi) + _dot(lower, mid) + _dot(lower, lo) + carry_ref[...]
    carry_ref[...] = d[n - 1:n, :]
    pieces = [jnp.where(lane == LANES - 1, 1.0, v.astype(f32)).astype(bf16) for v in _split3(d * LOG2E)]
    qaug = _dot_nt(selq_ref[0], pieces[0]) + _dot_nt(selq_ref[1], pieces[1]) + _dot_nt(selq_ref[2], pieces[2])
    kaug = _dot(pieces[0], selk_ref[0]) + _dot(pieces[1], selk_ref[1]) + _dot(pieces[2], selk_ref[2])
    qaug_ref[0] = qaug.astype(bf16)
    kaug_ref[0] = kaug.astype(bf16)


def _fox_selectors():
    selq = np.zeros((3, FOX_HEADS * AUG_ROWS, LANES), np.float32)
    selk = np.zeros((3, LANES, (FOX_HEADS // 2) * LANES), np.float32)
    one = LANES - 1
    for h in range(FOX_HEADS):
        pair, e = divmod(h, 2)
        for c in range(3):
            selq[c, h * AUG_ROWS + 6 * e + c, h] = 1.0
            selk[c, h, pair * LANES + 6 * e + 3 + c] = -1.0
        for c in range(3):
            selq[0, h * AUG_ROWS + 6 * e + 3 + c, one] = 1.0
            selk[0, one, pair * LANES + 6 * e + c] = 1.0
    return jnp.asarray(selq, bf16), jnp.asarray(selk, bf16)


def _fox_bias(lf):
    B, S, _ = lf.shape
    selq, selk = _fox_selectors()
    n = CUM_BLOCK
    nk = (FOX_HEADS // 2) * LANES
    return pl.pallas_call(
        _fox_bias_kernel, grid=(B, S // n),
        in_specs=[pl.BlockSpec((1, n, LANES), lambda b, j: (b, j, 0)), _const_spec(selq.shape), _const_spec(selk.shape)],
        out_specs=[pl.BlockSpec((1, FOX_HEADS * AUG_ROWS, n), lambda b, j: (b, 0, j)),
                   pl.BlockSpec((1, n, nk), lambda b, j: (b, j, 0))],
        out_shape=[jax.ShapeDtypeStruct((B, FOX_HEADS * AUG_ROWS, S), bf16),
                   jax.ShapeDtypeStruct((B, S, nk), bf16)],
        scratch_shapes=[pltpu.VMEM((1, LANES), f32)],
        compiler_params=_params("parallel", "arbitrary"), name="fox_bias",
    )(lf, selq, selk)


def _rotate_half_cols(w):
    half = w.shape[-1] // 2
    return jnp.concatenate([-w[..., half:], w[..., :half]], axis=-1)


def _pad_lanes(w):
    pad = [(0, 0)] * (w.ndim - 1) + [(0, LANES - w.shape[-1])]
    return jnp.pad(w, pad)


def _prep_layer_a(w_in, q_norm, w_uq, kv_norm, w_ukv):
    in_a = Q_LORA + KV_LORA + MLA_ROPE
    n = SB_HEADS * HEAD_DIM
    kr = w_in[:, Q_LORA + KV_LORA:in_a]
    sb = w_in[:, in_a:]
    uq = w_uq.reshape(Q_LORA, MLA_HEADS, MLA_NOPE + MLA_ROPE)
    rope = uq[:, :, MLA_NOPE:]
    ukv = w_ukv.reshape(KV_LORA, MLA_HEADS, MLA_NOPE + HEAD_DIM)
    return {
        "lat": w_in[:, :Q_LORA + KV_LORA].astype(bf16),
        "kr": jnp.concatenate([_pad_lanes(kr), _pad_lanes(_rotate_half_cols(kr))], axis=1).astype(bf16),
        "sbq_t": sb[:, :n].T.astype(bf16),
        "sbk": sb[:, n:2 * n].astype(bf16),
        "sbv_t": sb[:, 2 * n:].T.astype(bf16),
        "gq": q_norm.reshape(1, Q_LORA),
        "gkv": kv_norm.reshape(1, KV_LORA),
        "qn_t": uq[:, :, :MLA_NOPE].reshape(Q_LORA, -1).T.astype(bf16),
        "qra_t": rope.reshape(Q_LORA, -1).T.astype(bf16),
        "qrb_t": _rotate_half_cols(rope).reshape(Q_LORA, -1).T.astype(bf16),
        "kn": ukv[:, :, :MLA_NOPE].reshape(KV_LORA, -1).astype(bf16),
        "v_t": ukv[:, :, MLA_NOPE:].reshape(KV_LORA, -1).T.astype(bf16),
    }


def _prep_layer_c(w_in, b_f):
    n = FOX_HEADS * HEAD_DIM
    return {
        "q_t": w_in[:, :n].T.astype(bf16),
        "k": w_in[:, n:2 * n].astype(bf16),
        "v_t": w_in[:, 2 * n:3 * n].T.astype(bf16),
        "f": _pad_lanes(w_in[:, 3 * n:]).astype(bf16),
        "bf": _pad_lanes(b_f.reshape(1, FOX_HEADS)),
    }


def _rope_tables(seq):
    inv = 1.0 / (ROPE_BASE ** (jnp.arange(0, MLA_ROPE, 2, dtype=f32) / MLA_ROPE))
    ang = jnp.arange(seq, dtype=f32)[:, None] * inv[None, :]
    cos = _pad_lanes(jnp.concatenate([jnp.cos(ang), jnp.cos(ang)], axis=1))
    sin = _pad_lanes(jnp.concatenate([jnp.sin(ang), jnp.sin(ang)], axis=1))
    return cos, sin, cos[:, :MLA_ROPE].T, sin[:, :MLA_ROPE].T


def kernel(x, p, a_w_in, a_q_norm, a_w_uq, a_kv_norm, a_w_ukv, a_w_out, c_w_in, c_b_f, c_w_out,
           ffn_w1, ffn_w3, ffn_w2, ln1_g, ln1_b, ln2_g, ln2_b, ple_w_proj, ple_w_gate, ple_b_gate):
    B, S, D = x.shape
    M = B * S
    row = lambda v: v.reshape(1, -1)

    def channel_mixer(x1, i):
        return _ffn(x1, p[i].reshape(M, P_DIM), ffn_w1[i].astype(bf16), ffn_w3[i].astype(bf16),
                    ffn_w2[i].astype(bf16), row(ln2_g[i]), row(ln2_b[i]), ple_w_gate[i].astype(bf16),
                    row(ple_b_gate[i]), ple_w_proj[i].astype(bf16))

    wa = _prep_layer_a(a_w_in[0], a_q_norm[0], a_w_uq[0], a_kv_norm[0], a_w_ukv[0])
    cos, sin, cost, sint = _rope_tables(S)
    qnt, qrt, kn, kr, vat, qbt, kb, vbt = _proj_a(x, wa, cos, sin, cost, sint)
    o_a = _attn_softmax(qnt, qrt, kn, kr, vat, kx_shared=True, chunk_shift=SEQ_CHUNK.bit_length() - 1, name="attn_mla")
    o_b = _attn_sb(qbt, kb, vbt)
    na = MLA_HEADS * HEAD_DIM
    w_out = a_w_out[0].astype(bf16)
    x1 = _outproj([o_a.reshape(M, -1), o_b.reshape(M, -1)], [w_out[:na], w_out[na:]], x.reshape(M, D),
                  row(ln1_g[0]), row(ln1_b[0]))
    x2 = channel_mixer(x1, 0)

    wc = _prep_layer_c(c_w_in[0], c_b_f[0])
    qt, k, vt, lf = _proj_c(x2.reshape(B, S, D), wc)
    qaug, kaug = _fox_bias(lf)
    o_c = _attn_softmax(qt, qaug, k, kaug, vt, kx_shared=False, chunk_shift=0, name="attn_fox")
    x3 = _outproj([o_c.reshape(M, -1)], [c_w_out[0].astype(bf16)], x2, row(ln1_g[1]), row(ln1_b[1]))
    x4 = channel_mixer(x3, 1)
    return x4.reshape(B, S, D)
```

```python
import functools

import jax
import jax.numpy as jnp
import numpy as np
from jax import lax
from jax.experimental import pallas as pl
from jax.experimental.pallas import tpu as pltpu

D_MODEL = 1024
SEQ_CHUNK = 64
P_DIM = 256
MLA_HEADS = 8
MLA_NOPE = 64
MLA_ROPE = 32
Q_LORA = 256
KV_LORA = 256
ROPE_BASE = 10000.0
SB_HEADS = 8
FOX_HEADS = 16
HEAD_DIM = 64
DEPTH = 2
DEEPNORM_ALPHA = (2.0 * DEPTH) ** 0.25
LOG2E = 1.4426950408889634
MLA_SCALE = (MLA_NOPE + MLA_ROPE) ** -0.5 * LOG2E
HEAD_SCALE = HEAD_DIM ** -0.5 * LOG2E

LANES = 128
BF16_ROWS = 16
VMEM_LIMIT = 56 * 1024 * 1024

TOK_TILE = 512
TQ = 256
PAIRS_PER_STEP = 4
SCORE_LEAD = 1
TK_SOFTMAX = 256
TK_SB = 128
NEG = -1e30
AUG_ROWS = 16

bf16 = jnp.bfloat16
f32 = jnp.float32


def _dot(a, b):
    return jnp.dot(a, b, preferred_element_type=f32)


def _dot_nt(a, b):
    return lax.dot_general(a, b, (((1,), (1,)), ((), ())), preferred_element_type=f32)


def _layer_norm(z, g, b):
    mu = jnp.mean(z, axis=-1, keepdims=True)
    zc = z - mu
    var = jnp.mean(zc * zc, axis=-1, keepdims=True)
    return zc * lax.rsqrt(var + 1e-5) * g + b


def _rms_norm(c, g):
    return c * lax.rsqrt(jnp.mean(c * c, axis=-1, keepdims=True) + 1e-6) * g


def _log_sigmoid(a):
    return jnp.minimum(a, 0.0) - jnp.log(1.0 + jnp.exp(-jnp.abs(a)))


def _const_spec(shape):
    nd = len(shape)
    return pl.BlockSpec(shape, lambda *_: (0,) * nd, pipeline_mode=pl.Buffered(1))


def _params(*sem):
    return pltpu.CompilerParams(dimension_semantics=sem, vmem_limit_bytes=VMEM_LIMIT)


def _proj_a_kernel(x_ref, wlat_ref, wkr_ref, wsbq_ref, wsbk_ref, wsbv_ref, gq_ref, gkv_ref,
                   wqn_ref, wqra_ref, wqrb_ref, wkn_ref, wv_ref, cos_ref, sin_ref, cost_ref, sint_ref,
                   qnt_ref, qrt_ref, kn_ref, kr_ref, vat_ref, qbt_ref, kb_ref, vbt_ref):
    xb = x_ref[0].astype(bf16)
    lat = _dot(xb, wlat_ref[...])
    cq = _rms_norm(lat[:, :Q_LORA], gq_ref[...]).astype(bf16)
    ckv = _rms_norm(lat[:, Q_LORA:], gkv_ref[...]).astype(bf16)

    qnt_ref[0] = (_dot_nt(wqn_ref[...], cq) * MLA_SCALE).astype(bf16)
    qa = _dot_nt(wqra_ref[...], cq)
    qb = _dot_nt(wqrb_ref[...], cq)
    cost = cost_ref[...]
    sint = sint_ref[...]
    for h in range(MLA_HEADS):
        sl = slice(h * MLA_ROPE, (h + 1) * MLA_ROPE)
        qrt_ref[0, sl, :] = ((qa[sl] * cost + qb[sl] * sint) * MLA_SCALE).astype(bf16)

    kn_ref[0] = _dot(ckv, wkn_ref[...]).astype(bf16)
    kr2 = _dot(xb, wkr_ref[...])
    kr_ref[0] = (kr2[:, :LANES] * cos_ref[...] + kr2[:, LANES:] * sin_ref[...]).astype(bf16)

    vat = _dot_nt(wv_ref[...], ckv).astype(bf16)
    tm = vat.shape[1]
    for c in range(tm // TK_SOFTMAX):
        vat_ref[0, c] = vat[:, c * TK_SOFTMAX:(c + 1) * TK_SOFTMAX]

    qbt_ref[0] = (_dot_nt(wsbq_ref[...], xb) * HEAD_SCALE).astype(bf16)
    kb_ref[0] = _dot(xb, wsbk_ref[...]).astype(bf16)
    vbt = _dot_nt(wsbv_ref[...], xb).astype(bf16)
    for c in range(tm // TK_SB):
        vbt_ref[0, c] = vbt[:, c * TK_SB:(c + 1) * TK_SB]


def _proj_a(x, w, cos, sin, cost, sint):
    B, S, D = x.shape
    tm = TOK_TILE
    na = MLA_HEADS * HEAD_DIM
    nb = SB_HEADS * HEAD_DIM
    consts = [w["lat"], w["kr"], w["sbq_t"], w["sbk"], w["sbv_t"], w["gq"], w["gkv"],
              w["qn_t"], w["qra_t"], w["qrb_t"], w["kn"], w["v_t"]]
    in_specs = [pl.BlockSpec((1, tm, D), lambda b, i: (b, i, 0))]
    in_specs += [_const_spec(c.shape) for c in consts]
    in_specs += [pl.BlockSpec((tm, LANES), lambda b, i: (i, 0)),
                 pl.BlockSpec((tm, LANES), lambda b, i: (i, 0)),
                 pl.BlockSpec((MLA_ROPE, tm), lambda b, i: (0, i)),
                 pl.BlockSpec((MLA_ROPE, tm), lambda b, i: (0, i))]
    out_shape = [
        jax.ShapeDtypeStruct((B, na, S), bf16),
        jax.ShapeDtypeStruct((B, MLA_HEADS * MLA_ROPE, S), bf16),
        jax.ShapeDtypeStruct((B, S, na), bf16),
        jax.ShapeDtypeStruct((B, S, LANES), bf16),
        jax.ShapeDtypeStruct((B, S // TK_SOFTMAX, na, TK_SOFTMAX), bf16),
        jax.ShapeDtypeStruct((B, nb, S), bf16),
        jax.ShapeDtypeStruct((B, S, nb), bf16),
        jax.ShapeDtypeStruct((B, S // TK_SB, nb, TK_SB), bf16),
    ]
    out_specs = [
        pl.BlockSpec((1, na, tm), lambda b, i: (b, 0, i)),
        pl.BlockSpec((1, MLA_HEADS * MLA_ROPE, tm), lambda b, i: (b, 0, i)),
        pl.BlockSpec((1, tm, na), lambda b, i: (b, i, 0)),
        pl.BlockSpec((1, tm, LANES), lambda b, i: (b, i, 0)),
        pl.BlockSpec((1, tm // TK_SOFTMAX, na, TK_SOFTMAX), lambda b, i: (b, i, 0, 0)),
        pl.BlockSpec((1, nb, tm), lambda b, i: (b, 0, i)),
        pl.BlockSpec((1, tm, nb), lambda b, i: (b, i, 0)),
        pl.BlockSpec((1, tm // TK_SB, nb, TK_SB), lambda b, i: (b, i, 0, 0)),
    ]
    return pl.pallas_call(
        _proj_a_kernel, grid=(B, S // tm), in_specs=in_specs, out_specs=out_specs, out_shape=out_shape,
        compiler_params=_params("parallel", "parallel"), name="proj_a",
    )(x, *consts, cos, sin, cost, sint)


def _pair_queries(q1):
    row = lax.broadcasted_iota(jnp.int32, q1.shape, 0)
    zero = jnp.zeros_like(q1)
    return jnp.concatenate([jnp.where(row < HEAD_DIM, q1, zero), jnp.where(row >= HEAD_DIM, q1, zero)], axis=1)


def _positions(shape, key_start, query_start, tq):
    kpos = key_start + lax.broadcasted_iota(jnp.int32, shape, 0)
    col = lax.broadcasted_iota(jnp.int32, shape, 1)
    qpos = query_start + jnp.where(col >= tq, col - tq, col)
    return kpos, qpos


def _tile(c):
    return slice(c * LANES, (c + 1) * LANES)


def _chain_scratch(refs, per_chain):
    return [refs[c * per_chain:(c + 1) * per_chain] for c in range(PAIRS_PER_STEP)]


def _attn_softmax_kernel(q1_ref, qx_ref, k1_ref, kx_ref, vt_ref, o_ref, *scratch, tq, tk, chunk_shift, kx_shared):
    i = pl.program_id(2)
    chains = range(PAIRS_PER_STEP)
    xr = qx_ref.shape[1] // (2 * PAIRS_PER_STEP)
    q2t, m, acc0, acc1, sa, sb = zip(*_chain_scratch(scratch, 6))
    acc = (acc0, acc1)
    for c in chains:
        qx = jnp.concatenate([qx_ref[0, pl.ds(2 * c * xr, xr), :], qx_ref[0, pl.ds((2 * c + 1) * xr, xr), :]], axis=1)
        pad = jnp.zeros((LANES - xr, 2 * tq), bf16)
        q2t[c][...] = jnp.concatenate([_pair_queries(q1_ref[0, _tile(c), :]), qx, pad], axis=0)
        m[c][...] = jnp.full(m[c].shape, NEG, f32)
        for h in range(2):
            acc[h][c][...] = jnp.zeros(acc[h][c].shape, f32)
    ones = jnp.ones((BF16_ROWS, tk), bf16)

    def scores(j, s_refs, c):
        ks = pl.multiple_of(j * tk, tk)
        kx = kx_ref[0, pl.ds(ks, tk), _tile(0 if kx_shared else c)]
        kb = jnp.concatenate([k1_ref[0, pl.ds(ks, tk), _tile(c)], kx], axis=1)
        s_refs[c][...] = _dot(kb, q2t[c][...])

    def accumulate(j, s_refs, masked, j_next=None, next_refs=None):
        if masked:
            kpos, qpos = _positions((tk, 2 * tq), j * tk, i * tq, tq)
            ok = (kpos >> chunk_shift) <= (qpos >> chunk_shift)
        if next_refs is not None:
            for c in range(SCORE_LEAD):
                scores(j_next, next_refs, c)
        for c in chains:
            s = s_refs[c][...]
            if masked:
                s = jnp.where(ok, s, NEG)
            m_old = m[c][...]
            m_new = jnp.maximum(m_old, jnp.max(s, axis=0, keepdims=True))
            alpha = jnp.exp2(m_old - m_new)
            m[c][...] = m_new
            p = jnp.exp2(s - m_new).astype(bf16)
            for h in range(2):
                vta = jnp.concatenate([vt_ref[0, j, pl.ds(c * LANES + h * HEAD_DIM, HEAD_DIM), :], ones], axis=0)
                cols = slice(h * tq, (h + 1) * tq)
                acc[h][c][...] = alpha[:, cols] * acc[h][c][...] + _dot(vta, p[:, cols])
            if next_refs is not None and c + SCORE_LEAD < PAIRS_PER_STEP:
                scores(j_next, next_refs, c + SCORE_LEAD)

    for c in chains:
        scores(0, sa, c)

    def body(u, carry):
        accumulate(2 * u, sa, False, 2 * u + 1, sb)
        accumulate(2 * u + 1, sb, False, 2 * u + 2, sa)
        return carry

    lax.fori_loop(0, i // 2, body, 0)

    @pl.when(i % 2 == 0)
    def _():
        accumulate(i, sa, True)

    @pl.when(i % 2 == 1)
    def _():
        accumulate(i - 1, sa, False, i, sb)
        accumulate(i, sb, True)

    for c in chains:
        heads = []
        for h in range(2):
            a = acc[h][c][...]
            heads.append(a[:HEAD_DIM] * (1.0 / a[HEAD_DIM:HEAD_DIM + 1]))
        o_ref[0, :, _tile(c)] = jnp.concatenate(heads, axis=0).T.astype(o_ref.dtype)


def _attn_softmax(q1t, qxt, k1, kx, vt, *, kx_shared, chunk_shift, name):
    B, n1, S = q1t.shape
    C = PAIRS_PER_STEP
    groups = n1 // (C * LANES)
    tq, tk = TQ, TK_SOFTMAX
    assert tq == tk
    xr = qxt.shape[1] * LANES // (2 * n1)
    if kx_shared:
        kx_spec = pl.BlockSpec((1, S, LANES), lambda b, g, i: (b, 0, 0))
    else:
        kx_spec = pl.BlockSpec((1, S, C * LANES), lambda b, g, i: (b, 0, g))
    kern = functools.partial(_attn_softmax_kernel, tq=tq, tk=tk, chunk_shift=chunk_shift, kx_shared=kx_shared)
    per_chain = [pltpu.VMEM((2 * LANES, 2 * tq), bf16), pltpu.VMEM((1, 2 * tq), f32),
                 pltpu.VMEM((HEAD_DIM + BF16_ROWS, tq), f32), pltpu.VMEM((HEAD_DIM + BF16_ROWS, tq), f32),
                 pltpu.VMEM((tk, 2 * tq), f32), pltpu.VMEM((tk, 2 * tq), f32)]
    return pl.pallas_call(
        kern, grid=(B, groups, S // tq),
        in_specs=[pl.BlockSpec((1, C * LANES, tq), lambda b, g, i: (b, g, i)),
                  pl.BlockSpec((1, 2 * C * xr, tq), lambda b, g, i: (b, g, i)),
                  pl.BlockSpec((1, S, C * LANES), lambda b, g, i: (b, 0, g)),
                  kx_spec,
                  pl.BlockSpec((1, S // tk, C * LANES, tk), lambda b, g, i: (b, 0, g, 0))],
        out_specs=pl.BlockSpec((1, tq, C * LANES), lambda b, g, i: (b, i, g)),
        out_shape=jax.ShapeDtypeStruct((B, S, n1), bf16),
        scratch_shapes=C * per_chain,
        compiler_params=_params("parallel", "parallel", "arbitrary"), name=name,
    )(q1t, qxt, k1, kx, vt)


def _attn_sb_kernel(qt_ref, k_ref, vt_ref, o_ref, *scratch, tq, tk):
    i = pl.program_id(2)
    chains = range(PAIRS_PER_STEP)
    assert tq == 2 * tk
    last = 2 * i + 1
    q2t, r, acc0, acc1, z0, z1, lb0, lr0, f0, lb1, lr1, f1 = zip(*_chain_scratch(scratch, 12))
    acc = (acc0, acc1)
    zbuf = (z0, z1)
    sbuf = ((lb0, lr0, f0), (lb1, lr1, f1))
    for c in chains:
        q2t[c][...] = _pair_queries(qt_ref[0, _tile(c), :])
        r[c][...] = jnp.zeros(r[c].shape, f32)
        for h in range(2):
            acc[h][c][...] = jnp.zeros(acc[h][c].shape, f32)
    tri = (lax.broadcasted_iota(jnp.int32, (tk, tk), 0) < lax.broadcasted_iota(jnp.int32, (tk, tk), 1)).astype(bf16)

    def half(weights=None, logits=None, stats=None, masked=False):
        if weights is not None:
            t_w, par = weights
            lb, lr, first = sbuf[par]
            sums = [_dot(tri, lr[c][...]) for c in chains]
        if logits is not None:
            t_z, par = logits
            ks = pl.multiple_of((last - t_z) * tk, tk)
            for c in chains:
                zbuf[par][c][...] = _dot(k_ref[0, pl.ds(ks, tk), _tile(c)], q2t[c][...])
        if stats is not None:
            t_s, par = stats
            lb_s, lr_s, first_s = sbuf[par]
            if masked:
                kpos, qpos = _positions((tk, 2 * tq), (last - t_s) * tk, i * tq, tq)
                past = kpos < qpos
            for c in chains:
                z = zbuf[par][c][...]
                log_beta = jnp.minimum(z, 0.0) - jnp.log2(1.0 + jnp.exp2(-jnp.abs(z)))
                log_rem = log_beta - z
                if masked:
                    log_rem = jnp.where(past, log_rem, 0.0)
                    log_beta = jnp.where(past, log_beta, NEG)
                lr_s[c][...] = log_rem.astype(bf16)
                first_s[c][...] = log_rem[0:1, :]
                lb_s[c][...] = log_beta
        if weights is not None:
            att = []
            for c in chains:
                between = sums[c] + r[c][...]
                r[c][...] = between[0:1, :] + first[c][...]
                att.append(jnp.exp2(lb[c][...] + between).astype(bf16))
            for c in chains:
                for h in range(2):
                    cols = slice(h * tq, (h + 1) * tq)
                    vt = vt_ref[0, last - t_w, pl.ds(c * LANES + h * HEAD_DIM, HEAD_DIM), :]
                    acc[h][c][...] += _dot(vt, att[c][:, cols])

    half(logits=(0, 0))
    half(logits=(1, 1), stats=(0, 0), masked=True)
    half(weights=(0, 0), logits=(jnp.minimum(2, last), 0), stats=(1, 1), masked=True)

    def body(u, carry):
        half(weights=(2 * u - 1, 1), logits=(2 * u + 1, 1), stats=(2 * u, 0))
        half(weights=(2 * u, 0), logits=(2 * u + 2, 0), stats=(2 * u + 1, 1))
        return carry

    lax.fori_loop(1, i, body, 0)

    @pl.when(i >= 1)
    def _():
        half(weights=(2 * i - 1, 1), logits=(last, 1), stats=(2 * i, 0))
        half(weights=(2 * i, 0), stats=(last, 1))

    half(weights=(last, 1))
    for c in chains:
        o_ref[0, :, _tile(c)] = jnp.concatenate([acc0[c][...], acc1[c][...]], axis=0).T.astype(o_ref.dtype)


def _attn_sb(qt, k, vt):
    B, n1, S = qt.shape
    C = PAIRS_PER_STEP
    groups = n1 // (C * LANES)
    tq, tk = TQ, TK_SB
    kern = functools.partial(_attn_sb_kernel, tq=tq, tk=tk)
    return pl.pallas_call(
        kern, grid=(B, groups, S // tq),
        in_specs=[pl.BlockSpec((1, C * LANES, tq), lambda b, g, i: (b, g, i)),
                  pl.BlockSpec((1, S, C * LANES), lambda b, g, i: (b, 0, g)),
                  pl.BlockSpec((1, S // tk, C * LANES, tk), lambda b, g, i: (b, 0, g, 0))],
        out_specs=pl.BlockSpec((1, tq, C * LANES), lambda b, g, i: (b, i, g)),
        out_shape=jax.ShapeDtypeStruct((B, S, n1), bf16),
        scratch_shapes=C * (
            [pltpu.VMEM((LANES, 2 * tq), bf16), pltpu.VMEM((1, 2 * tq), f32),
             pltpu.VMEM((HEAD_DIM, tq), f32), pltpu.VMEM((HEAD_DIM, tq), f32)]
            + 2 * [pltpu.VMEM((tk, 2 * tq), f32)]
            + 2 * [pltpu.VMEM((tk, 2 * tq), f32), pltpu.VMEM((tk, 2 * tq), bf16), pltpu.VMEM((1, 2 * tq), f32)]),
        compiler_params=_params("parallel", "parallel", "arbitrary"), name="attn_sb",
    )(qt, k, vt)


def _layer_tail_kernel(*refs, n_parts):
    o_refs = refs[:n_parts]
    wo_refs = refs[n_parts:2 * n_parts]
    (x_ref, p_ref, g1_ref, b1_ref, w1_ref, w3_ref, w2_ref, g2_ref, b2_ref,
     wg_ref, bg_ref, wp_ref, y_ref) = refs[2 * n_parts:]
    mix = _dot(o_refs[0][...], wo_refs[0][...])
    for o_ref, w_ref in zip(o_refs[1:], wo_refs[1:]):
        mix = mix + _dot(o_ref[...], w_ref[...])
    x1 = _layer_norm(DEEPNORM_ALPHA * x_ref[...] + mix, g1_ref[...], b1_ref[...])
    xb = x1.astype(bf16)
    h1 = _dot(xb, w1_ref[...])
    h3 = _dot(xb, w3_ref[...])
    act = (h1 * jax.nn.sigmoid(h1) * h3).astype(bf16)
    x2 = _layer_norm(DEEPNORM_ALPHA * x1 + _dot(act, w2_ref[...]), g2_ref[...], b2_ref[...])
    gate = jax.nn.sigmoid(_dot(x2.astype(bf16), wg_ref[...]) + bg_ref[...])
    y_ref[...] = x2 + gate * _dot(p_ref[...].astype(bf16), wp_ref[...])


def _layer_tail(parts, wo_parts, x, p, consts):
    M, D = x.shape
    tm = TOK_TILE
    in_specs = [pl.BlockSpec((tm, o.shape[1]), lambda i: (i, 0)) for o in parts]
    in_specs += [_const_spec(w.shape) for w in wo_parts]
    in_specs += [pl.BlockSpec((tm, D), lambda i: (i, 0)), pl.BlockSpec((tm, p.shape[1]), lambda i: (i, 0))]
    in_specs += [_const_spec(c.shape) for c in consts]
    return pl.pallas_call(
        functools.partial(_layer_tail_kernel, n_parts=len(parts)), grid=(M // tm,), in_specs=in_specs,
        out_specs=pl.BlockSpec((tm, D), lambda i: (i, 0)),
        out_shape=jax.ShapeDtypeStruct((M, D), f32),
        compiler_params=_params("parallel"), name="layer_tail",
    )(*parts, *wo_parts, x, p, *consts)


def _proj_c_kernel(x_ref, wq_ref, wk_ref, wv_ref, wf_ref, bf_ref, qt_ref, k_ref, vt_ref, lf_ref):
    xb = x_ref[0].astype(bf16)
    qt_ref[0] = (_dot_nt(wq_ref[...], xb) * HEAD_SCALE).astype(bf16)
    k_ref[0] = _dot(xb, wk_ref[...]).astype(bf16)
    vt = _dot_nt(wv_ref[...], xb).astype(bf16)
    tm = vt.shape[1]
    for c in range(tm // TK_SOFTMAX):
        vt_ref[0, c] = vt[:, c * TK_SOFTMAX:(c + 1) * TK_SOFTMAX]
    lf_ref[0] = _log_sigmoid(_dot(xb, wf_ref[...]) + bf_ref[...])


def _proj_c(x, w):
    B, S, D = x.shape
    tm = TOK_TILE
    n = FOX_HEADS * HEAD_DIM
    consts = [w["q_t"], w["k"], w["v_t"], w["f"], w["bf"]]
    in_specs = [pl.BlockSpec((1, tm, D), lambda b, i: (b, i, 0))] + [_const_spec(c.shape) for c in consts]
    out_shape = [jax.ShapeDtypeStruct((B, n, S), bf16),
                 jax.ShapeDtypeStruct((B, S, n), bf16),
                 jax.ShapeDtypeStruct((B, S // TK_SOFTMAX, n, TK_SOFTMAX), bf16),
                 jax.ShapeDtypeStruct((B, S, LANES), f32)]
    out_specs = [pl.BlockSpec((1, n, tm), lambda b, i: (b, 0, i)),
                 pl.BlockSpec((1, tm, n), lambda b, i: (b, i, 0)),
                 pl.BlockSpec((1, tm // TK_SOFTMAX, n, TK_SOFTMAX), lambda b, i: (b, i, 0, 0)),
                 pl.BlockSpec((1, tm, LANES), lambda b, i: (b, i, 0))]
    return pl.pallas_call(
        _proj_c_kernel, grid=(B, S // tm), in_specs=in_specs, out_specs=out_specs, out_shape=out_shape,
        compiler_params=_params("parallel", "parallel"), name="proj_c",
    )(x, *consts)


CUM_BLOCK = 256


def _split3(v):
    hi = v.astype(bf16)
    r = v - hi.astype(f32)
    mid = r.astype(bf16)
    lo = (r - mid.astype(f32)).astype(bf16)
    return hi, mid, lo


def _fox_bias_kernel(lf_ref, selq_ref, selk_ref, qaug_ref, kaug_ref, carry_ref):
    n = CUM_BLOCK

    @pl.when(pl.program_id(1) == 0)
    def _():
        carry_ref[...] = jnp.zeros(carry_ref.shape, f32)

    lower = (lax.broadcasted_iota(jnp.int32, (n, n), 0) >= lax.broadcasted_iota(jnp.int32, (n, n), 1)).astype(bf16)
    lane = lax.broadcasted_iota(jnp.int32, (n, LANES), 1)
    hi, mid, lo = _split3(lf_ref[0])
    d = _dot(lower, hi) + _dot(lower, mid) + _dot(lower, lo) + carry_ref[...]
    carry_ref[...] = d[n - 1:n, :]
    pieces = [jnp.where(lane == LANES - 1, 1.0, v.astype(f32)).astype(bf16) for v in _split3(d * LOG2E)]
    qaug = _dot_nt(selq_ref[0], pieces[0]) + _dot_nt(selq_ref[1], pieces[1]) + _dot_nt(selq_ref[2], pieces[2])
    kaug = _dot(pieces[0], selk_ref[0]) + _dot(pieces[1], selk_ref[1]) + _dot(pieces[2], selk_ref[2])
    qaug_ref[0] = qaug.astype(bf16)
    kaug_ref[0] = kaug.astype(bf16)


def _fox_selectors():
    selq = np.zeros((3, FOX_HEADS * AUG_ROWS, LANES), np.float32)
    selk = np.zeros((3, LANES, (FOX_HEADS // 2) * LANES), np.float32)
    one = LANES - 1
    for h in range(FOX_HEADS):
        pair, e = divmod(h, 2)
        for c in range(3):
            selq[c, h * AUG_ROWS + 6 * e + c, h] = 1.0
            selk[c, h, pair * LANES + 6 * e + 3 + c] = -1.0
        for c in range(3):
            selq[0, h * AUG_ROWS + 6 * e + 3 + c, one] = 1.0
            selk[0, one, pair * LANES + 6 * e + c] = 1.0
    return jnp.asarray(selq, bf16), jnp.asarray(selk, bf16)


def _fox_bias(lf):
    B, S, _ = lf.shape
    selq, selk = _fox_selectors()
    n = CUM_BLOCK
    nk = (FOX_HEADS // 2) * LANES
    return pl.pallas_call(
        _fox_bias_kernel, grid=(B, S // n),
        in_specs=[pl.BlockSpec((1, n, LANES), lambda b, j: (b, j, 0)), _const_spec(selq.shape), _const_spec(selk.shape)],
        out_specs=[pl.BlockSpec((1, FOX_HEADS * AUG_ROWS, n), lambda b, j: (b, 0, j)),
                   pl.BlockSpec((1, n, nk), lambda b, j: (b, j, 0))],
        out_shape=[jax.ShapeDtypeStruct((B, FOX_HEADS * AUG_ROWS, S), bf16),
                   jax.ShapeDtypeStruct((B, S, nk), bf16)],
        scratch_shapes=[pltpu.VMEM((1, LANES), f32)],
        compiler_params=_params("parallel", "arbitrary"), name="fox_bias",
    )(lf, selq, selk)


def _rotate_half_cols(w):
    half = w.shape[-1] // 2
    return jnp.concatenate([-w[..., half:], w[..., :half]], axis=-1)


def _pad_lanes(w):
    pad = [(0, 0)] * (w.ndim - 1) + [(0, LANES - w.shape[-1])]
    return jnp.pad(w, pad)


def _prep_layer_a(w_in, q_norm, w_uq, kv_norm, w_ukv):
    in_a = Q_LORA + KV_LORA + MLA_ROPE
    n = SB_HEADS * HEAD_DIM
    kr = w_in[:, Q_LORA + KV_LORA:in_a]
    sb = w_in[:, in_a:]
    uq = w_uq.reshape(Q_LORA, MLA_HEADS, MLA_NOPE + MLA_ROPE)
    rope = uq[:, :, MLA_NOPE:]
    ukv = w_ukv.reshape(KV_LORA, MLA_HEADS, MLA_NOPE + HEAD_DIM)
    return {
        "lat": w_in[:, :Q_LORA + KV_LORA].astype(bf16),
        "kr": jnp.concatenate([_pad_lanes(kr), _pad_lanes(_rotate_half_cols(kr))], axis=1).astype(bf16),
        "sbq_t": sb[:, :n].T.astype(bf16),
        "sbk": sb[:, n:2 * n].astype(bf16),
        "sbv_t": sb[:, 2 * n:].T.astype(bf16),
        "gq": q_norm.reshape(1, Q_LORA),
        "gkv": kv_norm.reshape(1, KV_LORA),
        "qn_t": uq[:, :, :MLA_NOPE].reshape(Q_LORA, -1).T.astype(bf16),
        "qra_t": rope.reshape(Q_LORA, -1).T.astype(bf16),
        "qrb_t": _rotate_half_cols(rope).reshape(Q_LORA, -1).T.astype(bf16),
        "kn": ukv[:, :, :MLA_NOPE].reshape(KV_LORA, -1).astype(bf16),
        "v_t": ukv[:, :, MLA_NOPE:].reshape(KV_LORA, -1).T.astype(bf16),
    }


def _prep_layer_c(w_in, b_f):
    n = FOX_HEADS * HEAD_DIM
    return {
        "q_t": w_in[:, :n].T.astype(bf16),
        "k": w_in[:, n:2 * n].astype(bf16),
        "v_t": w_in[:, 2 * n:3 * n].T.astype(bf16),
        "f": _pad_lanes(w_in[:, 3 * n:]).astype(bf16),
        "bf": _pad_lanes(b_f.reshape(1, FOX_HEADS)),
    }


def _rope_tables(seq):
    inv = 1.0 / (ROPE_BASE ** (jnp.arange(0, MLA_ROPE, 2, dtype=f32) / MLA_ROPE))
    ang = jnp.arange(seq, dtype=f32)[:, None] * inv[None, :]
    cos = _pad_lanes(jnp.concatenate([jnp.cos(ang), jnp.cos(ang)], axis=1))
    sin = _pad_lanes(jnp.concatenate([jnp.sin(ang), jnp.sin(ang)], axis=1))
    return cos, sin, cos[:, :MLA_ROPE].T, sin[:, :MLA_ROPE].T


def kernel(x, p, a_w_in, a_q_norm, a_w_uq, a_kv_norm, a_w_ukv, a_w_out, c_w_in, c_b_f, c_w_out,
           ffn_w1, ffn_w3, ffn_w2, ln1_g, ln1_b, ln2_g, ln2_b, ple_w_proj, ple_w_gate, ple_b_gate):
    B, S, D = x.shape
    M = B * S
    row = lambda v: v.reshape(1, -1)

    def layer_tail(parts, wo_parts, xin, i):
        consts = [row(ln1_g[i]), row(ln1_b[i]), ffn_w1[i].astype(bf16), ffn_w3[i].astype(bf16),
                  ffn_w2[i].astype(bf16), row(ln2_g[i]), row(ln2_b[i]), ple_w_gate[i].astype(bf16),
                  row(ple_b_gate[i]), ple_w_proj[i].astype(bf16)]
        return _layer_tail(parts, wo_parts, xin, p[i].reshape(M, P_DIM), consts)

    wa = _prep_layer_a(a_w_in[0], a_q_norm[0], a_w_uq[0], a_kv_norm[0], a_w_ukv[0])
    cos, sin, cost, sint = _rope_tables(S)
    qnt, qrt, kn, kr, vat, qbt, kb, vbt = _proj_a(x, wa, cos, sin, cost, sint)
    o_a = _attn_softmax(qnt, qrt, kn, kr, vat, kx_shared=True, chunk_shift=SEQ_CHUNK.bit_length() - 1, name="attn_mla")
    o_b = _attn_sb(qbt, kb, vbt)
    na = MLA_HEADS * HEAD_DIM
    w_out = a_w_out[0].astype(bf16)
    x2 = layer_tail([o_a.reshape(M, -1), o_b.reshape(M, -1)], [w_out[:na], w_out[na:]], x.reshape(M, D), 0)

    wc = _prep_layer_c(c_w_in[0], c_b_f[0])
    qt, k, vt, lf = _proj_c(x2.reshape(B, S, D), wc)
    qaug, kaug = _fox_bias(lf)
    o_c = _attn_softmax(qt, qaug, k, kaug, vt, kx_shared=False, chunk_shift=0, name="attn_fox")
    x4 = layer_tail([o_c.reshape(M, -1)], [c_w_out[0].astype(bf16)], x2, 1)
    return x4.reshape(B, S, D)
```

```python
import functools

import jax
import jax.numpy as jnp
import numpy as np
from jax import lax
from jax.experimental import pallas as pl
from jax.experimental.pallas import tpu as pltpu

D_MODEL = 1024
SEQ_CHUNK = 64
P_DIM = 256
MLA_HEADS = 8
MLA_NOPE = 64
MLA_ROPE = 32
Q_LORA = 256
KV_LORA = 256
ROPE_BASE = 10000.0
SB_HEADS = 8
FOX_HEADS = 16
HEAD_DIM = 64
DEPTH = 2
DEEPNORM_ALPHA = (2.0 * DEPTH) ** 0.25
LOG2E = 1.4426950408889634
MLA_SCALE = (MLA_NOPE + MLA_ROPE) ** -0.5 * LOG2E
HEAD_SCALE = HEAD_DIM ** -0.5 * LOG2E

LANES = 128
BF16_ROWS = 16
VMEM_LIMIT = 56 * 1024 * 1024

TOK_TILE = 512
TQ = 256
PAIRS_PER_STEP = 4
SCORE_LEAD = 1
TK_SOFTMAX = 256
TK_SB = 128
NEG = -1e30
AUG_ROWS = 16

bf16 = jnp.bfloat16
f32 = jnp.float32


def _dot(a, b):
    return jnp.dot(a, b, preferred_element_type=f32)


def _dot_nt(a, b):
    return lax.dot_general(a, b, (((1,), (1,)), ((), ())), preferred_element_type=f32)


def _layer_norm(z, g, b):
    mu = jnp.mean(z, axis=-1, keepdims=True)
    zc = z - mu
    var = jnp.mean(zc * zc, axis=-1, keepdims=True)
    return zc * lax.rsqrt(var + 1e-5) * g + b


def _rms_norm(c, g):
    return c * lax.rsqrt(jnp.mean(c * c, axis=-1, keepdims=True) + 1e-6) * g


def _log_sigmoid(a):
    return jnp.minimum(a, 0.0) - jnp.log(1.0 + jnp.exp(-jnp.abs(a)))


def _const_spec(shape):
    nd = len(shape)
    return pl.BlockSpec(shape, lambda *_: (0,) * nd, pipeline_mode=pl.Buffered(1))


def _params(*sem):
    return pltpu.CompilerParams(dimension_semantics=sem, vmem_limit_bytes=VMEM_LIMIT)


def _proj_a_kernel(x_ref, wlat_ref, wkr_ref, wsbq_ref, wsbk_ref, wsbv_ref, gq_ref, gkv_ref,
                   wqn_ref, wqra_ref, wqrb_ref, wkn_ref, wv_ref, cos_ref, sin_ref, cost_ref, sint_ref,
                   qnt_ref, qrt_ref, kn_ref, kr_ref, vat_ref, qbt_ref, kb_ref, vbt_ref):
    xb = x_ref[0].astype(bf16)
    lat = _dot(xb, wlat_ref[...])
    cq = _rms_norm(lat[:, :Q_LORA], gq_ref[...]).astype(bf16)
    ckv = _rms_norm(lat[:, Q_LORA:], gkv_ref[...]).astype(bf16)

    qnt_ref[0] = (_dot_nt(wqn_ref[...], cq) * MLA_SCALE).astype(bf16)
    qa = _dot_nt(wqra_ref[...], cq)
    qb = _dot_nt(wqrb_ref[...], cq)
    cost = cost_ref[...]
    sint = sint_ref[...]
    for h in range(MLA_HEADS):
        sl = slice(h * MLA_ROPE, (h + 1) * MLA_ROPE)
        qrt_ref[0, sl, :] = ((qa[sl] * cost + qb[sl] * sint) * MLA_SCALE).astype(bf16)

    kn_ref[0] = _dot(ckv, wkn_ref[...]).astype(bf16)
    kr2 = _dot(xb, wkr_ref[...])
    kr_ref[0] = (kr2[:, :LANES] * cos_ref[...] + kr2[:, LANES:] * sin_ref[...]).astype(bf16)

    vat = _dot_nt(wv_ref[...], ckv).astype(bf16)
    tm = vat.shape[1]
    for c in range(tm // TK_SOFTMAX):
        vat_ref[0, c] = vat[:, c * TK_SOFTMAX:(c + 1) * TK_SOFTMAX]

    qbt_ref[0] = (_dot_nt(wsbq_ref[...], xb) * HEAD_SCALE).astype(bf16)
    kb_ref[0] = _dot(xb, wsbk_ref[...]).astype(bf16)
    vbt = _dot_nt(wsbv_ref[...], xb).astype(bf16)
    for c in range(tm // TK_SB):
        vbt_ref[0, c] = vbt[:, c * TK_SB:(c + 1) * TK_SB]


def _proj_a(x, w, cos, sin, cost, sint):
    B, S, D = x.shape
    tm = TOK_TILE
    na = MLA_HEADS * HEAD_DIM
    nb = SB_HEADS * HEAD_DIM
    consts = [w["lat"], w["kr"], w["sbq_t"], w["sbk"], w["sbv_t"], w["gq"], w["gkv"],
              w["qn_t"], w["qra_t"], w["qrb_t"], w["kn"], w["v_t"]]
    in_specs = [pl.BlockSpec((1, tm, D), lambda b, i: (b, i, 0))]
    in_specs += [_const_spec(c.shape) for c in consts]
    in_specs += [pl.BlockSpec((tm, LANES), lambda b, i: (i, 0)),
                 pl.BlockSpec((tm, LANES), lambda b, i: (i, 0)),
                 pl.BlockSpec((MLA_ROPE, tm), lambda b, i: (0, i)),
                 pl.BlockSpec((MLA_ROPE, tm), lambda b, i: (0, i))]
    out_shape = [
        jax.ShapeDtypeStruct((B, na, S), bf16),
        jax.ShapeDtypeStruct((B, MLA_HEADS * MLA_ROPE, S), bf16),
        jax.ShapeDtypeStruct((B, S, na), bf16),
        jax.ShapeDtypeStruct((B, S, LANES), bf16),
        jax.ShapeDtypeStruct((B, S // TK_SOFTMAX, na, TK_SOFTMAX), bf16),
        jax.ShapeDtypeStruct((B, nb, S), bf16),
        jax.ShapeDtypeStruct((B, S, nb), bf16),
        jax.ShapeDtypeStruct((B, S // TK_SB, nb, TK_SB), bf16),
    ]
    out_specs = [
        pl.BlockSpec((1, na, tm), lambda b, i: (b, 0, i)),
        pl.BlockSpec((1, MLA_HEADS * MLA_ROPE, tm), lambda b, i: (b, 0, i)),
        pl.BlockSpec((1, tm, na), lambda b, i: (b, i, 0)),
        pl.BlockSpec((1, tm, LANES), lambda b, i: (b, i, 0)),
        pl.BlockSpec((1, tm // TK_SOFTMAX, na, TK_SOFTMAX), lambda b, i: (b, i, 0, 0)),
        pl.BlockSpec((1, nb, tm), lambda b, i: (b, 0, i)),
        pl.BlockSpec((1, tm, nb), lambda b, i: (b, i, 0)),
        pl.BlockSpec((1, tm // TK_SB, nb, TK_SB), lambda b, i: (b, i, 0, 0)),
    ]
    return pl.pallas_call(
        _proj_a_kernel, grid=(B, S // tm), in_specs=in_specs, out_specs=out_specs, out_shape=out_shape,
        compiler_params=_params("parallel", "parallel"), name="proj_a",
    )(x, *consts, cos, sin, cost, sint)


def _pair_queries(q1):
    row = lax.broadcasted_iota(jnp.int32, q1.shape, 0)
    zero = jnp.zeros_like(q1)
    return jnp.concatenate([jnp.where(row < HEAD_DIM, q1, zero), jnp.where(row >= HEAD_DIM, q1, zero)], axis=1)


def _positions(shape, key_start, query_start, tq):
    kpos = key_start + lax.broadcasted_iota(jnp.int32, shape, 0)
    col = lax.broadcasted_iota(jnp.int32, shape, 1)
    qpos = query_start + jnp.where(col >= tq, col - tq, col)
    return kpos, qpos


def _tile(c):
    return slice(c * LANES, (c + 1) * LANES)


def _loop_by_two(start, stop, body):
    n = stop - start

    def two(v, carry):
        body(start + 2 * v)
        body(start + 2 * v + 1)
        return carry

    lax.fori_loop(0, n // 2, two, 0)

    @pl.when(n % 2 == 1)
    def _():
        body(stop - 1)


def _chain_scratch(refs, per_chain):
    return [refs[c * per_chain:(c + 1) * per_chain] for c in range(PAIRS_PER_STEP)]


def _attn_softmax_kernel(q1_ref, qx_ref, k1_ref, kx_ref, vt_ref, o_ref, *scratch, tq, tk, chunk_shift, kx_shared):
    i = pl.program_id(2)
    chains = range(PAIRS_PER_STEP)
    xr = qx_ref.shape[1] // (2 * PAIRS_PER_STEP)
    q2t, m, acc0, acc1, sa, sb = zip(*_chain_scratch(scratch, 6))
    acc = (acc0, acc1)
    for c in chains:
        qx = jnp.concatenate([qx_ref[0, pl.ds(2 * c * xr, xr), :], qx_ref[0, pl.ds((2 * c + 1) * xr, xr), :]], axis=1)
        pad = jnp.zeros((LANES - xr, 2 * tq), bf16)
        q2t[c][...] = jnp.concatenate([_pair_queries(q1_ref[0, _tile(c), :]), qx, pad], axis=0)
        m[c][...] = jnp.full(m[c].shape, NEG, f32)
        for h in range(2):
            acc[h][c][...] = jnp.zeros(acc[h][c].shape, f32)
    ones = jnp.ones((BF16_ROWS, tk), bf16)

    def scores(j, s_refs, c):
        ks = pl.multiple_of(j * tk, tk)
        kx = kx_ref[0, pl.ds(ks, tk), _tile(0 if kx_shared else c)]
        kb = jnp.concatenate([k1_ref[0, pl.ds(ks, tk), _tile(c)], kx], axis=1)
        s_refs[c][...] = _dot(kb, q2t[c][...])

    def accumulate(j, s_refs, masked, j_next=None, next_refs=None):
        if masked:
            kpos, qpos = _positions((tk, 2 * tq), j * tk, i * tq, tq)
            ok = (kpos >> chunk_shift) <= (qpos >> chunk_shift)
        if next_refs is not None:
            for c in range(SCORE_LEAD):
                scores(j_next, next_refs, c)
        for c in chains:
            s = s_refs[c][...]
            if masked:
                s = jnp.where(ok, s, NEG)
            m_old = m[c][...]
            m_new = jnp.maximum(m_old, jnp.max(s, axis=0, keepdims=True))
            alpha = jnp.exp2(m_old - m_new)
            m[c][...] = m_new
            p = jnp.exp2(s - m_new).astype(bf16)
            for h in range(2):
                vta = jnp.concatenate([vt_ref[0, j, pl.ds(c * LANES + h * HEAD_DIM, HEAD_DIM), :], ones], axis=0)
                cols = slice(h * tq, (h + 1) * tq)
                acc[h][c][...] = alpha[:, cols] * acc[h][c][...] + _dot(vta, p[:, cols])
            if next_refs is not None and c + SCORE_LEAD < PAIRS_PER_STEP:
                scores(j_next, next_refs, c + SCORE_LEAD)

    for c in chains:
        scores(0, sa, c)

    def body(u):
        accumulate(2 * u, sa, False, 2 * u + 1, sb)
        accumulate(2 * u + 1, sb, False, 2 * u + 2, sa)

    _loop_by_two(0, i // 2, body)

    @pl.when(i % 2 == 0)
    def _():
        accumulate(i, sa, True)

    @pl.when(i % 2 == 1)
    def _():
        accumulate(i - 1, sa, False, i, sb)
        accumulate(i, sb, True)

    for c in chains:
        heads = []
        for h in range(2):
            a = acc[h][c][...]
            heads.append(a[:HEAD_DIM] * (1.0 / a[HEAD_DIM:HEAD_DIM + 1]))
        o_ref[0, :, _tile(c)] = jnp.concatenate(heads, axis=0).T.astype(o_ref.dtype)


def _attn_softmax(q1t, qxt, k1, kx, vt, *, kx_shared, chunk_shift, name):
    B, n1, S = q1t.shape
    C = PAIRS_PER_STEP
    groups = n1 // (C * LANES)
    tq, tk = TQ, TK_SOFTMAX
    assert tq == tk
    xr = qxt.shape[1] * LANES // (2 * n1)
    if kx_shared:
        kx_spec = pl.BlockSpec((1, S, LANES), lambda b, g, i: (b, 0, 0))
    else:
        kx_spec = pl.BlockSpec((1, S, C * LANES), lambda b, g, i: (b, 0, g))
    kern = functools.partial(_attn_softmax_kernel, tq=tq, tk=tk, chunk_shift=chunk_shift, kx_shared=kx_shared)
    per_chain = [pltpu.VMEM((2 * LANES, 2 * tq), bf16), pltpu.VMEM((1, 2 * tq), f32),
                 pltpu.VMEM((HEAD_DIM + BF16_ROWS, tq), f32), pltpu.VMEM((HEAD_DIM + BF16_ROWS, tq), f32),
                 pltpu.VMEM((tk, 2 * tq), f32), pltpu.VMEM((tk, 2 * tq), f32)]
    return pl.pallas_call(
        kern, grid=(B, groups, S // tq),
        in_specs=[pl.BlockSpec((1, C * LANES, tq), lambda b, g, i: (b, g, i)),
                  pl.BlockSpec((1, 2 * C * xr, tq), lambda b, g, i: (b, g, i)),
                  pl.BlockSpec((1, S, C * LANES), lambda b, g, i: (b, 0, g)),
                  kx_spec,
                  pl.BlockSpec((1, S // tk, C * LANES, tk), lambda b, g, i: (b, 0, g, 0))],
        out_specs=pl.BlockSpec((1, tq, C * LANES), lambda b, g, i: (b, i, g)),
        out_shape=jax.ShapeDtypeStruct((B, S, n1), bf16),
        scratch_shapes=C * per_chain,
        compiler_params=_params("parallel", "parallel", "arbitrary"), name=name,
    )(q1t, qxt, k1, kx, vt)


def _attn_sb_kernel(qt_ref, k_ref, vt_ref, o_ref, *scratch, tq, tk):
    i = pl.program_id(2)
    chains = range(PAIRS_PER_STEP)
    assert tq == 2 * tk
    last = 2 * i + 1
    q2t, r, acc0, acc1, z0, z1, lb0, lr0, f0, lb1, lr1, f1 = zip(*_chain_scratch(scratch, 12))
    acc = (acc0, acc1)
    zbuf = (z0, z1)
    sbuf = ((lb0, lr0, f0), (lb1, lr1, f1))
    for c in chains:
        q2t[c][...] = _pair_queries(qt_ref[0, _tile(c), :])
        r[c][...] = jnp.zeros(r[c].shape, f32)
        for h in range(2):
            acc[h][c][...] = jnp.zeros(acc[h][c].shape, f32)
    tri = (lax.broadcasted_iota(jnp.int32, (tk, tk), 0) < lax.broadcasted_iota(jnp.int32, (tk, tk), 1)).astype(bf16)

    def half(weights=None, logits=None, stats=None, masked=False):
        if weights is not None:
            t_w, par = weights
            lb, lr, first = sbuf[par]
            sums = [_dot(tri, lr[c][...]) for c in chains]
        if logits is not None:
            t_z, par = logits
            ks = pl.multiple_of((last - t_z) * tk, tk)
            for c in chains:
                zbuf[par][c][...] = _dot(k_ref[0, pl.ds(ks, tk), _tile(c)], q2t[c][...])
        if stats is not None:
            t_s, par = stats
            lb_s, lr_s, first_s = sbuf[par]
            if masked:
                kpos, qpos = _positions((tk, 2 * tq), (last - t_s) * tk, i * tq, tq)
                past = kpos < qpos
            for c in chains:
                z = zbuf[par][c][...]
                log_beta = jnp.minimum(z, 0.0) - jnp.log2(1.0 + jnp.exp2(-jnp.abs(z)))
                log_rem = log_beta - z
                if masked:
                    log_rem = jnp.where(past, log_rem, 0.0)
                    log_beta = jnp.where(past, log_beta, NEG)
                lr_s[c][...] = log_rem.astype(bf16)
                first_s[c][...] = log_rem[0:1, :]
                lb_s[c][...] = log_beta
        if weights is not None:
            att = []
            for c in chains:
                between = sums[c] + r[c][...]
                r[c][...] = between[0:1, :] + first[c][...]
                att.append(jnp.exp2(lb[c][...] + between).astype(bf16))
            for c in chains:
                for h in range(2):
                    cols = slice(h * tq, (h + 1) * tq)
                    vt = vt_ref[0, last - t_w, pl.ds(c * LANES + h * HEAD_DIM, HEAD_DIM), :]
                    acc[h][c][...] += _dot(vt, att[c][:, cols])

    half(logits=(0, 0))
    half(logits=(1, 1), stats=(0, 0), masked=True)
    half(weights=(0, 0), logits=(jnp.minimum(2, last), 0), stats=(1, 1), masked=True)

    def body(u, carry):
        half(weights=(2 * u - 1, 1), logits=(2 * u + 1, 1), stats=(2 * u, 0))
        half(weights=(2 * u, 0), logits=(2 * u + 2, 0), stats=(2 * u + 1, 1))
        return carry

    lax.fori_loop(1, i, body, 0)

    @pl.when(i >= 1)
    def _():
        half(weights=(2 * i - 1, 1), logits=(last, 1), stats=(2 * i, 0))
        half(weights=(2 * i, 0), stats=(last, 1))

    half(weights=(last, 1))
    for c in chains:
        o_ref[0, :, _tile(c)] = jnp.concatenate([acc0[c][...], acc1[c][...]], axis=0).T.astype(o_ref.dtype)


def _attn_sb(qt, k, vt):
    B, n1, S = qt.shape
    C = PAIRS_PER_STEP
    groups = n1 // (C * LANES)
    tq, tk = TQ, TK_SB
    kern = functools.partial(_attn_sb_kernel, tq=tq, tk=tk)
    return pl.pallas_call(
        kern, grid=(B, groups, S // tq),
        in_specs=[pl.BlockSpec((1, C * LANES, tq), lambda b, g, i: (b, g, i)),
                  pl.BlockSpec((1, S, C * LANES), lambda b, g, i: (b, 0, g)),
                  pl.BlockSpec((1, S // tk, C * LANES, tk), lambda b, g, i: (b, 0, g, 0))],
        out_specs=pl.BlockSpec((1, tq, C * LANES), lambda b, g, i: (b, i, g)),
        out_shape=jax.ShapeDtypeStruct((B, S, n1), bf16),
        scratch_shapes=C * (
            [pltpu.VMEM((LANES, 2 * tq), bf16), pltpu.VMEM((1, 2 * tq), f32),
             pltpu.VMEM((HEAD_DIM, tq), f32), pltpu.VMEM((HEAD_DIM, tq), f32)]
            + 2 * [pltpu.VMEM((tk, 2 * tq), f32)]
            + 2 * [pltpu.VMEM((tk, 2 * tq), f32), pltpu.VMEM((tk, 2 * tq), bf16), pltpu.VMEM((1, 2 * tq), f32)]),
        compiler_params=_params("parallel", "parallel", "arbitrary"), name="attn_sb",
    )(qt, k, vt)


def _layer_tail_kernel(*refs, n_parts):
    o_refs = refs[:n_parts]
    wo_refs = refs[n_parts:2 * n_parts]
    (x_ref, p_ref, g1_ref, b1_ref, w1_ref, w3_ref, w2_ref, g2_ref, b2_ref,
     wg_ref, bg_ref, wp_ref, y_ref) = refs[2 * n_parts:]
    mix = _dot(o_refs[0][...], wo_refs[0][...])
    for o_ref, w_ref in zip(o_refs[1:], wo_refs[1:]):
        mix = mix + _dot(o_ref[...], w_ref[...])
    x1 = _layer_norm(DEEPNORM_ALPHA * x_ref[...] + mix, g1_ref[...], b1_ref[...])
    xb = x1.astype(bf16)
    h1 = _dot(xb, w1_ref[...])
    h3 = _dot(xb, w3_ref[...])
    act = (h1 * jax.nn.sigmoid(h1) * h3).astype(bf16)
    x2 = _layer_norm(DEEPNORM_ALPHA * x1 + _dot(act, w2_ref[...]), g2_ref[...], b2_ref[...])
    gate = jax.nn.sigmoid(_dot(x2.astype(bf16), wg_ref[...]) + bg_ref[...])
    y_ref[...] = x2 + gate * _dot(p_ref[...].astype(bf16), wp_ref[...])


def _layer_tail(parts, wo_parts, x, p, consts):
    M, D = x.shape
    tm = TOK_TILE
    in_specs = [pl.BlockSpec((tm, o.shape[1]), lambda i: (i, 0)) for o in parts]
    in_specs += [_const_spec(w.shape) for w in wo_parts]
    in_specs += [pl.BlockSpec((tm, D), lambda i: (i, 0)), pl.BlockSpec((tm, p.shape[1]), lambda i: (i, 0))]
    in_specs += [_const_spec(c.shape) for c in consts]
    return pl.pallas_call(
        functools.partial(_layer_tail_kernel, n_parts=len(parts)), grid=(M // tm,), in_specs=in_specs,
        out_specs=pl.BlockSpec((tm, D), lambda i: (i, 0)),
        out_shape=jax.ShapeDtypeStruct((M, D), f32),
        compiler_params=_params("parallel"), name="layer_tail",
    )(*parts, *wo_parts, x, p, *consts)


def _proj_c_kernel(x_ref, wq_ref, wk_ref, wv_ref, wf_ref, bf_ref, qt_ref, k_ref, vt_ref, lf_ref):
    xb = x_ref[0].astype(bf16)
    qt_ref[0] = (_dot_nt(wq_ref[...], xb) * HEAD_SCALE).astype(bf16)
    k_ref[0] = _dot(xb, wk_ref[...]).astype(bf16)
    vt = _dot_nt(wv_ref[...], xb).astype(bf16)
    tm = vt.shape[1]
    for c in range(tm // TK_SOFTMAX):
        vt_ref[0, c] = vt[:, c * TK_SOFTMAX:(c + 1) * TK_SOFTMAX]
    lf_ref[0] = _log_sigmoid(_dot(xb, wf_ref[...]) + bf_ref[...])


def _proj_c(x, w):
    B, S, D = x.shape
    tm = TOK_TILE
    n = FOX_HEADS * HEAD_DIM
    consts = [w["q_t"], w["k"], w["v_t"], w["f"], w["bf"]]
    in_specs = [pl.BlockSpec((1, tm, D), lambda b, i: (b, i, 0))] + [_const_spec(c.shape) for c in consts]
    out_shape = [jax.ShapeDtypeStruct((B, n, S), bf16),
                 jax.ShapeDtypeStruct((B, S, n), bf16),
                 jax.ShapeDtypeStruct((B, S // TK_SOFTMAX, n, TK_SOFTMAX), bf16),
                 jax.ShapeDtypeStruct((B, S, LANES), f32)]
    out_specs = [pl.BlockSpec((1, n, tm), lambda b, i: (b, 0, i)),
                 pl.BlockSpec((1, tm, n), lambda b, i: (b, i, 0)),
                 pl.BlockSpec((1, tm // TK_SOFTMAX, n, TK_SOFTMAX), lambda b, i: (b, i, 0, 0)),
                 pl.BlockSpec((1, tm, LANES), lambda b, i: (b, i, 0))]
    return pl.pallas_call(
        _proj_c_kernel, grid=(B, S // tm), in_specs=in_specs, out_specs=out_specs, out_shape=out_shape,
        compiler_params=_params("parallel", "parallel"), name="proj_c",
    )(x, *consts)


CUM_BLOCK = 256


def _split3(v):
    hi = v.astype(bf16)
    r = v - hi.astype(f32)
    mid = r.astype(bf16)
    lo = (r - mid.astype(f32)).astype(bf16)
    return hi, mid, lo


def _fox_bias_kernel(lf_ref, selq_ref, selk_ref, qaug_ref, kaug_ref, carry_ref):
    n = CUM_BLOCK

    @pl.when(pl.program_id(1) == 0)
    def _():
        carry_ref[...] = jnp.zeros(carry_ref.shape, f32)

    lower = (lax.broadcasted_iota(jnp.int32, (n, n), 0) >= lax.broadcasted_iota(jnp.int32, (n, n), 1)).astype(bf16)
    lane = lax.broadcasted_iota(jnp.int32, (n, LANES), 1)
    hi, mid, lo = _split3(lf_ref[0])
    d = _dot(lower, hi) + _dot(lower, mid) + _dot(lower, lo) + carry_ref[...]
    carry_ref[...] = d[n - 1:n, :]
    pieces = [jnp.where(lane == LANES - 1, 1.0, v.astype(f32)).astype(bf16) for v in _split3(d * LOG2E)]
    qaug = _dot_nt(selq_ref[0], pieces[0]) + _dot_nt(selq_ref[1], pieces[1]) + _dot_nt(selq_ref[2], pieces[2])
    kaug = _dot(pieces[0], selk_ref[0]) + _dot(pieces[1], selk_ref[1]) + _dot(pieces[2], selk_ref[2])
    qaug_ref[0] = qaug.astype(bf16)
    kaug_ref[0] = kaug.astype(bf16)


def _fox_selectors():
    selq = np.zeros((3, FOX_HEADS * AUG_ROWS, LANES), np.float32)
    selk = np.zeros((3, LANES, (FOX_HEADS // 2) * LANES), np.float32)
    one = LANES - 1
    for h in range(FOX_HEADS):
        pair, e = divmod(h, 2)
        for c in range(3):
            selq[c, h * AUG_ROWS + 6 * e + c, h] = 1.0
            selk[c, h, pair * LANES + 6 * e + 3 + c] = -1.0
        for c in range(3):
            selq[0, h * AUG_ROWS + 6 * e + 3 + c, one] = 1.0
            selk[0, one, pair * LANES + 6 * e + c] = 1.0
    return jnp.asarray(selq, bf16), jnp.asarray(selk, bf16)


def _fox_bias(lf):
    B, S, _ = lf.shape
    selq, selk = _fox_selectors()
    n = CUM_BLOCK
    nk = (FOX_HEADS // 2) * LANES
    return pl.pallas_call(
        _fox_bias_kernel, grid=(B, S // n),
        in_specs=[pl.BlockSpec((1, n, LANES), lambda b, j: (b, j, 0)), _const_spec(selq.shape), _const_spec(selk.shape)],
        out_specs=[pl.BlockSpec((1, FOX_HEADS * AUG_ROWS, n), lambda b, j: (b, 0, j)),
                   pl.BlockSpec((1, n, nk), lambda b, j: (b, j, 0))],
        out_shape=[jax.ShapeDtypeStruct((B, FOX_HEADS * AUG_ROWS, S), bf16),
                   jax.ShapeDtypeStruct((B, S, nk), bf16)],
        scratch_shapes=[pltpu.VMEM((1, LANES), f32)],
        compiler_params=_params("parallel", "arbitrary"), name="fox_bias",
    )(lf, selq, selk)


def _rotate_half_cols(w):
    half = w.shape[-1] // 2
    return jnp.concatenate([-w[..., half:], w[..., :half]], axis=-1)


def _pad_lanes(w):
    pad = [(0, 0)] * (w.ndim - 1) + [(0, LANES - w.shape[-1])]
    return jnp.pad(w, pad)


def _prep_layer_a(w_in, q_norm, w_uq, kv_norm, w_ukv):
    in_a = Q_LORA + KV_LORA + MLA_ROPE
    n = SB_HEADS * HEAD_DIM
    kr = w_in[:, Q_LORA + KV_LORA:in_a]
    sb = w_in[:, in_a:]
    uq = w_uq.reshape(Q_LORA, MLA_HEADS, MLA_NOPE + MLA_ROPE)
    rope = uq[:, :, MLA_NOPE:]
    ukv = w_ukv.reshape(KV_LORA, MLA_HEADS, MLA_NOPE + HEAD_DIM)
    return {
        "lat": w_in[:, :Q_LORA + KV_LORA].astype(bf16),
        "kr": jnp.concatenate([_pad_lanes(kr), _pad_lanes(_rotate_half_cols(kr))], axis=1).astype(bf16),
        "sbq_t": sb[:, :n].T.astype(bf16),
        "sbk": sb[:, n:2 * n].astype(bf16),
        "sbv_t": sb[:, 2 * n:].T.astype(bf16),
        "gq": q_norm.reshape(1, Q_LORA),
        "gkv": kv_norm.reshape(1, KV_LORA),
        "qn_t": uq[:, :, :MLA_NOPE].reshape(Q_LORA, -1).T.astype(bf16),
        "qra_t": rope.reshape(Q_LORA, -1).T.astype(bf16),
        "qrb_t": _rotate_half_cols(rope).reshape(Q_LORA, -1).T.astype(bf16),
        "kn": ukv[:, :, :MLA_NOPE].reshape(KV_LORA, -1).astype(bf16),
        "v_t": ukv[:, :, MLA_NOPE:].reshape(KV_LORA, -1).T.astype(bf16),
    }


def _prep_layer_c(w_in, b_f):
    n = FOX_HEADS * HEAD_DIM
    return {
        "q_t": w_in[:, :n].T.astype(bf16),
        "k": w_in[:, n:2 * n].astype(bf16),
        "v_t": w_in[:, 2 * n:3 * n].T.astype(bf16),
        "f": _pad_lanes(w_in[:, 3 * n:]).astype(bf16),
        "bf": _pad_lanes(b_f.reshape(1, FOX_HEADS)),
    }


def _rope_tables(seq):
    inv = 1.0 / (ROPE_BASE ** (jnp.arange(0, MLA_ROPE, 2, dtype=f32) / MLA_ROPE))
    ang = jnp.arange(seq, dtype=f32)[:, None] * inv[None, :]
    cos = _pad_lanes(jnp.concatenate([jnp.cos(ang), jnp.cos(ang)], axis=1))
    sin = _pad_lanes(jnp.concatenate([jnp.sin(ang), jnp.sin(ang)], axis=1))
    return cos, sin, cos[:, :MLA_ROPE].T, sin[:, :MLA_ROPE].T


def kernel(x, p, a_w_in, a_q_norm, a_w_uq, a_kv_norm, a_w_ukv, a_w_out, c_w_in, c_b_f, c_w_out,
           ffn_w1, ffn_w3, ffn_w2, ln1_g, ln1_b, ln2_g, ln2_b, ple_w_proj, ple_w_gate, ple_b_gate):
    B, S, D = x.shape
    M = B * S
    row = lambda v: v.reshape(1, -1)

    def layer_tail(parts, wo_parts, xin, i):
        consts = [row(ln1_g[i]), row(ln1_b[i]), ffn_w1[i].astype(bf16), ffn_w3[i].astype(bf16),
                  ffn_w2[i].astype(bf16), row(ln2_g[i]), row(ln2_b[i]), ple_w_gate[i].astype(bf16),
                  row(ple_b_gate[i]), ple_w_proj[i].astype(bf16)]
        return _layer_tail(parts, wo_parts, xin, p[i].reshape(M, P_DIM), consts)

    wa = _prep_layer_a(a_w_in[0], a_q_norm[0], a_w_uq[0], a_kv_norm[0], a_w_ukv[0])
    cos, sin, cost, sint = _rope_tables(S)
    qnt, qrt, kn, kr, vat, qbt, kb, vbt = _proj_a(x, wa, cos, sin, cost, sint)
    o_a = _attn_softmax(qnt, qrt, kn, kr, vat, kx_shared=True, chunk_shift=SEQ_CHUNK.bit_length() - 1, name="attn_mla")
    o_b = _attn_sb(qbt, kb, vbt)
    na = MLA_HEADS * HEAD_DIM
    w_out = a_w_out[0].astype(bf16)
    x2 = layer_tail([o_a.reshape(M, -1), o_b.reshape(M, -1)], [w_out[:na], w_out[na:]], x.reshape(M, D), 0)

    wc = _prep_layer_c(c_w_in[0], c_b_f[0])
    qt, k, vt, lf = _proj_c(x2.reshape(B, S, D), wc)
    qaug, kaug = _fox_bias(lf)
    o_c = _attn_softmax(qt, qaug, k, kaug, vt, kx_shared=False, chunk_shift=0, name="attn_fox")
    x4 = layer_tail([o_c.reshape(M, -1)], [c_w_out[0].astype(bf16)], x2, 1)
    return x4.reshape(B, S, D)
```

```python
import functools

import jax
import jax.numpy as jnp
import numpy as np
from jax import lax
from jax.experimental import pallas as pl
from jax.experimental.pallas import tpu as pltpu

D_MODEL = 1024
SEQ_CHUNK = 64
P_DIM = 256
MLA_HEADS = 8
MLA_NOPE = 64
MLA_ROPE = 32
Q_LORA = 256
KV_LORA = 256
ROPE_BASE = 10000.0
SB_HEADS = 8
FOX_HEADS = 16
HEAD_DIM = 64
DEPTH = 2
DEEPNORM_ALPHA = (2.0 * DEPTH) ** 0.25
LOG2E = 1.4426950408889634
MLA_SCALE = (MLA_NOPE + MLA_ROPE) ** -0.5 * LOG2E
HEAD_SCALE = HEAD_DIM ** -0.5 * LOG2E

LANES = 128
BF16_ROWS = 16
VMEM_LIMIT = 56 * 1024 * 1024

TOK_TILE = 512
TQ = 256
PAIRS_PER_STEP = 4
SCORE_LEAD = 1
TK_SOFTMAX = 256
TK_SB = 128
NEG = -1e30
AUG_ROWS = 16

bf16 = jnp.bfloat16
f32 = jnp.float32


def _dot(a, b):
    return jnp.dot(a, b, preferred_element_type=f32)


def _dot_nt(a, b):
    return lax.dot_general(a, b, (((1,), (1,)), ((), ())), preferred_element_type=f32)


def _layer_norm(z, g, b):
    mu = jnp.mean(z, axis=-1, keepdims=True)
    zc = z - mu
    var = jnp.mean(zc * zc, axis=-1, keepdims=True)
    return zc * lax.rsqrt(var + 1e-5) * g + b


def _rms_norm(c, g):
    return c * lax.rsqrt(jnp.mean(c * c, axis=-1, keepdims=True) + 1e-6) * g


def _log_sigmoid(a):
    return jnp.minimum(a, 0.0) - jnp.log(1.0 + jnp.exp(-jnp.abs(a)))


def _const_spec(shape):
    nd = len(shape)
    return pl.BlockSpec(shape, lambda *_: (0,) * nd, pipeline_mode=pl.Buffered(1))


def _params(*sem):
    return pltpu.CompilerParams(dimension_semantics=sem, vmem_limit_bytes=VMEM_LIMIT)


def _proj_a_kernel(x_ref, wlat_ref, wkr_ref, wsbq_ref, wsbk_ref, wsbv_ref, gq_ref, gkv_ref,
                   wqn_ref, wqra_ref, wqrb_ref, wkn_ref, wv_ref, cos_ref, sin_ref, cost_ref, sint_ref,
                   qnt_ref, qrt_ref, kn_ref, kr_ref, vat_ref, qbt_ref, kb_ref, vbt_ref):
    xb = x_ref[0].astype(bf16)
    lat = _dot(xb, wlat_ref[...])
    cq = _rms_norm(lat[:, :Q_LORA], gq_ref[...]).astype(bf16)
    ckv = _rms_norm(lat[:, Q_LORA:], gkv_ref[...]).astype(bf16)

    qnt_ref[0] = (_dot_nt(wqn_ref[...], cq) * MLA_SCALE).astype(bf16)
    qa = _dot_nt(wqra_ref[...], cq)
    qb = _dot_nt(wqrb_ref[...], cq)
    cost = cost_ref[...]
    sint = sint_ref[...]
    for h in range(MLA_HEADS):
        sl = slice(h * MLA_ROPE, (h + 1) * MLA_ROPE)
        qrt_ref[0, sl, :] = ((qa[sl] * cost + qb[sl] * sint) * MLA_SCALE).astype(bf16)

    kn_ref[0] = _dot(ckv, wkn_ref[...]).astype(bf16)
    kr2 = _dot(xb, wkr_ref[...])
    kr_ref[0] = (kr2[:, :LANES] * cos_ref[...] + kr2[:, LANES:] * sin_ref[...]).astype(bf16)

    vat = _dot_nt(wv_ref[...], ckv).astype(bf16)
    tm = vat.shape[1]
    for c in range(tm // TK_SOFTMAX):
        vat_ref[0, c] = vat[:, c * TK_SOFTMAX:(c + 1) * TK_SOFTMAX]

    qbt_ref[0] = (_dot_nt(wsbq_ref[...], xb) * HEAD_SCALE).astype(bf16)
    kb_ref[0] = _dot(xb, wsbk_ref[...]).astype(bf16)
    vbt = _dot_nt(wsbv_ref[...], xb).astype(bf16)
    for c in range(tm // TK_SB):
        vbt_ref[0, c] = vbt[:, c * TK_SB:(c + 1) * TK_SB]


def _proj_a(x, w, cos, sin, cost, sint):
    B, S, D = x.shape
    tm = TOK_TILE
    na = MLA_HEADS * HEAD_DIM
    nb = SB_HEADS * HEAD_DIM
    consts = [w["lat"], w["kr"], w["sbq_t"], w["sbk"], w["sbv_t"], w["gq"], w["gkv"],
              w["qn_t"], w["qra_t"], w["qrb_t"], w["kn"], w["v_t"]]
    in_specs = [pl.BlockSpec((1, tm, D), lambda b, i: (b, i, 0))]
    in_specs += [_const_spec(c.shape) for c in consts]
    in_specs += [pl.BlockSpec((tm, LANES), lambda b, i: (i, 0)),
                 pl.BlockSpec((tm, LANES), lambda b, i: (i, 0)),
                 pl.BlockSpec((MLA_ROPE, tm), lambda b, i: (0, i)),
                 pl.BlockSpec((MLA_ROPE, tm), lambda b, i: (0, i))]
    out_shape = [
        jax.ShapeDtypeStruct((B, na, S), bf16),
        jax.ShapeDtypeStruct((B, MLA_HEADS * MLA_ROPE, S), bf16),
        jax.ShapeDtypeStruct((B, S, na), bf16),
        jax.ShapeDtypeStruct((B, S, LANES), bf16),
        jax.ShapeDtypeStruct((B, S // TK_SOFTMAX, na, TK_SOFTMAX), bf16),
        jax.ShapeDtypeStruct((B, nb, S), bf16),
        jax.ShapeDtypeStruct((B, S, nb), bf16),
        jax.ShapeDtypeStruct((B, S // TK_SB, nb, TK_SB), bf16),
    ]
    out_specs = [
        pl.BlockSpec((1, na, tm), lambda b, i: (b, 0, i)),
        pl.BlockSpec((1, MLA_HEADS * MLA_ROPE, tm), lambda b, i: (b, 0, i)),
        pl.BlockSpec((1, tm, na), lambda b, i: (b, i, 0)),
        pl.BlockSpec((1, tm, LANES), lambda b, i: (b, i, 0)),
        pl.BlockSpec((1, tm // TK_SOFTMAX, na, TK_SOFTMAX), lambda b, i: (b, i, 0, 0)),
        pl.BlockSpec((1, nb, tm), lambda b, i: (b, 0, i)),
        pl.BlockSpec((1, tm, nb), lambda b, i: (b, i, 0)),
        pl.BlockSpec((1, tm // TK_SB, nb, TK_SB), lambda b, i: (b, i, 0, 0)),
    ]
    return pl.pallas_call(
        _proj_a_kernel, grid=(B, S // tm), in_specs=in_specs, out_specs=out_specs, out_shape=out_shape,
        compiler_params=_params("parallel", "parallel"), name="proj_a",
    )(x, *consts, cos, sin, cost, sint)


def _pair_queries(q1):
    row = lax.broadcasted_iota(jnp.int32, q1.shape, 0)
    zero = jnp.zeros_like(q1)
    return jnp.concatenate([jnp.where(row < HEAD_DIM, q1, zero), jnp.where(row >= HEAD_DIM, q1, zero)], axis=1)


def _positions(shape, key_start, query_start, tq):
    kpos = key_start + lax.broadcasted_iota(jnp.int32, shape, 0)
    col = lax.broadcasted_iota(jnp.int32, shape, 1)
    qpos = query_start + jnp.where(col >= tq, col - tq, col)
    return kpos, qpos


def _tile(c):
    return slice(c * LANES, (c + 1) * LANES)


def _loop_by_two(start, stop, body):
    n = stop - start

    def two(v, carry):
        body(start + 2 * v)
        body(start + 2 * v + 1)
        return carry

    lax.fori_loop(0, n // 2, two, 0)

    @pl.when(n % 2 == 1)
    def _():
        body(stop - 1)


def _chain_scratch(refs, per_chain):
    return [refs[c * per_chain:(c + 1) * per_chain] for c in range(PAIRS_PER_STEP)]


def _attn_softmax_kernel(q1_ref, qx_ref, k1_ref, kx_ref, vt_ref, o_ref, *scratch, tq, tk, chunk_shift, kx_shared):
    i = pl.program_id(2)
    chains = range(PAIRS_PER_STEP)
    xr = qx_ref.shape[1] // (2 * PAIRS_PER_STEP)
    q2t, m, acc0, acc1, sa, sb = zip(*_chain_scratch(scratch, 6))
    acc = (acc0, acc1)
    for c in chains:
        qx = jnp.concatenate([qx_ref[0, pl.ds(2 * c * xr, xr), :], qx_ref[0, pl.ds((2 * c + 1) * xr, xr), :]], axis=1)
        q2t[c][...] = jnp.concatenate([_pair_queries(q1_ref[0, _tile(c), :]), qx], axis=0)
        m[c][...] = jnp.full(m[c].shape, NEG, f32)
        for h in range(2):
            acc[h][c][...] = jnp.zeros(acc[h][c].shape, f32)
    ones = jnp.ones((BF16_ROWS, tk), bf16)

    def scores(j, s_refs, c):
        ks = pl.multiple_of(j * tk, tk)
        kx = kx_ref[0, pl.ds(ks, tk), pl.ds((0 if kx_shared else c) * LANES, xr)]
        kb = jnp.concatenate([k1_ref[0, pl.ds(ks, tk), _tile(c)], kx], axis=1)
        s_refs[c][...] = _dot(kb, q2t[c][...])

    def accumulate(j, s_refs, masked, j_next=None, next_refs=None):
        if masked:
            kpos, qpos = _positions((tk, 2 * tq), j * tk, i * tq, tq)
            ok = (kpos >> chunk_shift) <= (qpos >> chunk_shift)
        if next_refs is not None:
            for c in range(SCORE_LEAD):
                scores(j_next, next_refs, c)
        for c in chains:
            s = s_refs[c][...]
            if masked:
                s = jnp.where(ok, s, NEG)
            m_old = m[c][...]
            m_new = jnp.maximum(m_old, jnp.max(s, axis=0, keepdims=True))
            alpha = jnp.exp2(m_old - m_new)
            m[c][...] = m_new
            p = jnp.exp2(s - m_new).astype(bf16)
            for h in range(2):
                vta = jnp.concatenate([vt_ref[0, j, pl.ds(c * LANES + h * HEAD_DIM, HEAD_DIM), :], ones], axis=0)
                cols = slice(h * tq, (h + 1) * tq)
                acc[h][c][...] = alpha[:, cols] * acc[h][c][...] + _dot(vta, p[:, cols])
            if next_refs is not None and c + SCORE_LEAD < PAIRS_PER_STEP:
                scores(j_next, next_refs, c + SCORE_LEAD)

    for c in chains:
        scores(0, sa, c)

    def body(u):
        accumulate(2 * u, sa, False, 2 * u + 1, sb)
        accumulate(2 * u + 1, sb, False, 2 * u + 2, sa)

    _loop_by_two(0, i // 2, body)

    @pl.when(i % 2 == 0)
    def _():
        accumulate(i, sa, True)

    @pl.when(i % 2 == 1)
    def _():
        accumulate(i - 1, sa, False, i, sb)
        accumulate(i, sb, True)

    for c in chains:
        heads = []
        for h in range(2):
            a = acc[h][c][...]
            heads.append(a[:HEAD_DIM] * (1.0 / a[HEAD_DIM:HEAD_DIM + 1]))
        o_ref[0, :, _tile(c)] = jnp.concatenate(heads, axis=0).T.astype(o_ref.dtype)


def _attn_softmax(q1t, qxt, k1, kx, vt, *, kx_shared, chunk_shift, name):
    B, n1, S = q1t.shape
    C = PAIRS_PER_STEP
    groups = n1 // (C * LANES)
    tq, tk = TQ, TK_SOFTMAX
    assert tq == tk
    xr = qxt.shape[1] * LANES // (2 * n1)
    if kx_shared:
        kx_spec = pl.BlockSpec((1, S, LANES), lambda b, g, i: (b, 0, 0))
    else:
        kx_spec = pl.BlockSpec((1, S, C * LANES), lambda b, g, i: (b, 0, g))
    kern = functools.partial(_attn_softmax_kernel, tq=tq, tk=tk, chunk_shift=chunk_shift, kx_shared=kx_shared)
    per_chain = [pltpu.VMEM((LANES + xr, 2 * tq), bf16), pltpu.VMEM((1, 2 * tq), f32),
                 pltpu.VMEM((HEAD_DIM + BF16_ROWS, tq), f32), pltpu.VMEM((HEAD_DIM + BF16_ROWS, tq), f32),
                 pltpu.VMEM((tk, 2 * tq), f32), pltpu.VMEM((tk, 2 * tq), f32)]
    return pl.pallas_call(
        kern, grid=(B, groups, S // tq),
        in_specs=[pl.BlockSpec((1, C * LANES, tq), lambda b, g, i: (b, g, i)),
                  pl.BlockSpec((1, 2 * C * xr, tq), lambda b, g, i: (b, g, i)),
                  pl.BlockSpec((1, S, C * LANES), lambda b, g, i: (b, 0, g)),
                  kx_spec,
                  pl.BlockSpec((1, S // tk, C * LANES, tk), lambda b, g, i: (b, 0, g, 0))],
        out_specs=pl.BlockSpec((1, tq, C * LANES), lambda b, g, i: (b, i, g)),
        out_shape=jax.ShapeDtypeStruct((B, S, n1), bf16),
        scratch_shapes=C * per_chain,
        compiler_params=_params("parallel", "parallel", "arbitrary"), name=name,
    )(q1t, qxt, k1, kx, vt)


def _attn_sb_kernel(qt_ref, k_ref, vt_ref, o_ref, *scratch, tq, tk):
    i = pl.program_id(2)
    chains = range(PAIRS_PER_STEP)
    assert tq == 2 * tk
    last = 2 * i + 1
    q2t, r, acc0, acc1, z0, z1, lb0, lr0, f0, lb1, lr1, f1 = zip(*_chain_scratch(scratch, 12))
    acc = (acc0, acc1)
    zbuf = (z0, z1)
    sbuf = ((lb0, lr0, f0), (lb1, lr1, f1))
    for c in chains:
        q2t[c][...] = _pair_queries(qt_ref[0, _tile(c), :])
        r[c][...] = jnp.zeros(r[c].shape, f32)
        for h in range(2):
            acc[h][c][...] = jnp.zeros(acc[h][c].shape, f32)
    tri = (lax.broadcasted_iota(jnp.int32, (tk, tk), 0) < lax.broadcasted_iota(jnp.int32, (tk, tk), 1)).astype(bf16)

    def half(weights=None, logits=None, stats=None, masked=False):
        if weights is not None:
            t_w, par = weights
            lb, lr, first = sbuf[par]
            sums = [_dot(tri, lr[c][...]) for c in chains]
        if logits is not None:
            t_z, par = logits
            ks = pl.multiple_of((last - t_z) * tk, tk)
            for c in chains:
                zbuf[par][c][...] = _dot(k_ref[0, pl.ds(ks, tk), _tile(c)], q2t[c][...])
        if stats is not None:
            t_s, par = stats
            lb_s, lr_s, first_s = sbuf[par]
            if masked:
                kpos, qpos = _positions((tk, 2 * tq), (last - t_s) * tk, i * tq, tq)
                past = kpos < qpos
            for c in chains:
                z = zbuf[par][c][...]
                log_beta = jnp.minimum(z, 0.0) - jnp.log2(1.0 + jnp.exp2(-jnp.abs(z)))
                log_rem = log_beta - z
                if masked:
                    log_rem = jnp.where(past, log_rem, 0.0)
                    log_beta = jnp.where(past, log_beta, NEG)
                lr_s[c][...] = log_rem.astype(bf16)
                first_s[c][...] = log_rem[0:1, :]
                lb_s[c][...] = log_beta
        if weights is not None:
            att = []
            for c in chains:
                between = sums[c] + r[c][...]
                r[c][...] = between[0:1, :] + first[c][...]
                att.append(jnp.exp2(lb[c][...] + between).astype(bf16))
            for c in chains:
                for h in range(2):
                    cols = slice(h * tq, (h + 1) * tq)
                    vt = vt_ref[0, last - t_w, pl.ds(c * LANES + h * HEAD_DIM, HEAD_DIM), :]
                    acc[h][c][...] += _dot(vt, att[c][:, cols])

    half(logits=(0, 0))
    half(logits=(1, 1), stats=(0, 0), masked=True)
    half(weights=(0, 0), logits=(jnp.minimum(2, last), 0), stats=(1, 1), masked=True)

    def body(u, carry):
        half(weights=(2 * u - 1, 1), logits=(2 * u + 1, 1), stats=(2 * u, 0))
        half(weights=(2 * u, 0), logits=(2 * u + 2, 0), stats=(2 * u + 1, 1))
        return carry

    lax.fori_loop(1, i, body, 0)

    @pl.when(i >= 1)
    def _():
        half(weights=(2 * i - 1, 1), logits=(last, 1), stats=(2 * i, 0))
        half(weights=(2 * i, 0), stats=(last, 1))

    half(weights=(last, 1))
    for c in chains:
        o_ref[0, :, _tile(c)] = jnp.concatenate([acc0[c][...], acc1[c][...]], axis=0).T.astype(o_ref.dtype)


def _attn_sb(qt, k, vt):
    B, n1, S = qt.shape
    C = PAIRS_PER_STEP
    groups = n1 // (C * LANES)
    tq, tk = TQ, TK_SB
    kern = functools.partial(_attn_sb_kernel, tq=tq, tk=tk)
    return pl.pallas_call(
        kern, grid=(B, groups, S // tq),
        in_specs=[pl.BlockSpec((1, C * LANES, tq), lambda b, g, i: (b, g, i)),
                  pl.BlockSpec((1, S, C * LANES), lambda b, g, i: (b, 0, g)),
                  pl.BlockSpec((1, S // tk, C * LANES, tk), lambda b, g, i: (b, 0, g, 0))],
        out_specs=pl.BlockSpec((1, tq, C * LANES), lambda b, g, i: (b, i, g)),
        out_shape=jax.ShapeDtypeStruct((B, S, n1), bf16),
        scratch_shapes=C * (
            [pltpu.VMEM((LANES, 2 * tq), bf16), pltpu.VMEM((1, 2 * tq), f32),
             pltpu.VMEM((HEAD_DIM, tq), f32), pltpu.VMEM((HEAD_DIM, tq), f32)]
            + 2 * [pltpu.VMEM((tk, 2 * tq), f32)]
            + 2 * [pltpu.VMEM((tk, 2 * tq), f32), pltpu.VMEM((tk, 2 * tq), bf16), pltpu.VMEM((1, 2 * tq), f32)]),
        compiler_params=_params("parallel", "parallel", "arbitrary"), name="attn_sb",
    )(qt, k, vt)


def _layer_tail_kernel(*refs, n_parts):
    o_refs = refs[:n_parts]
    wo_refs = refs[n_parts:2 * n_parts]
    (x_ref, p_ref, g1_ref, b1_ref, w1_ref, w3_ref, w2_ref, g2_ref, b2_ref,
     wg_ref, bg_ref, wp_ref, y_ref) = refs[2 * n_parts:]
    mix = _dot(o_refs[0][...], wo_refs[0][...])
    for o_ref, w_ref in zip(o_refs[1:], wo_refs[1:]):
        mix = mix + _dot(o_ref[...], w_ref[...])
    x1 = _layer_norm(DEEPNORM_ALPHA * x_ref[...] + mix, g1_ref[...], b1_ref[...])
    xb = x1.astype(bf16)
    h1 = _dot(xb, w1_ref[...])
    h3 = _dot(xb, w3_ref[...])
    act = (h1 * jax.nn.sigmoid(h1) * h3).astype(bf16)
    x2 = _layer_norm(DEEPNORM_ALPHA * x1 + _dot(act, w2_ref[...]), g2_ref[...], b2_ref[...])
    gate = jax.nn.sigmoid(_dot(x2.astype(bf16), wg_ref[...]) + bg_ref[...])
    y_ref[...] = x2 + gate * _dot(p_ref[...].astype(bf16), wp_ref[...])


def _layer_tail(parts, wo_parts, x, p, consts):
    M, D = x.shape
    tm = TOK_TILE
    in_specs = [pl.BlockSpec((tm, o.shape[1]), lambda i: (i, 0)) for o in parts]
    in_specs += [_const_spec(w.shape) for w in wo_parts]
    in_specs += [pl.BlockSpec((tm, D), lambda i: (i, 0)), pl.BlockSpec((tm, p.shape[1]), lambda i: (i, 0))]
    in_specs += [_const_spec(c.shape) for c in consts]
    return pl.pallas_call(
        functools.partial(_layer_tail_kernel, n_parts=len(parts)), grid=(M // tm,), in_specs=in_specs,
        out_specs=pl.BlockSpec((tm, D), lambda i: (i, 0)),
        out_shape=jax.ShapeDtypeStruct((M, D), f32),
        compiler_params=_params("parallel"), name="layer_tail",
    )(*parts, *wo_parts, x, p, *consts)


def _proj_c_kernel(x_ref, wq_ref, wk_ref, wv_ref, wf_ref, bf_ref, qt_ref, k_ref, vt_ref, lf_ref):
    xb = x_ref[0].astype(bf16)
    qt_ref[0] = (_dot_nt(wq_ref[...], xb) * HEAD_SCALE).astype(bf16)
    k_ref[0] = _dot(xb, wk_ref[...]).astype(bf16)
    vt = _dot_nt(wv_ref[...], xb).astype(bf16)
    tm = vt.shape[1]
    for c in range(tm // TK_SOFTMAX):
        vt_ref[0, c] = vt[:, c * TK_SOFTMAX:(c + 1) * TK_SOFTMAX]
    lf_ref[0] = _log_sigmoid(_dot(xb, wf_ref[...]) + bf_ref[...])


def _proj_c(x, w):
    B, S, D = x.shape
    tm = TOK_TILE
    n = FOX_HEADS * HEAD_DIM
    consts = [w["q_t"], w["k"], w["v_t"], w["f"], w["bf"]]
    in_specs = [pl.BlockSpec((1, tm, D), lambda b, i: (b, i, 0))] + [_const_spec(c.shape) for c in consts]
    out_shape = [jax.ShapeDtypeStruct((B, n, S), bf16),
                 jax.ShapeDtypeStruct((B, S, n), bf16),
                 jax.ShapeDtypeStruct((B, S // TK_SOFTMAX, n, TK_SOFTMAX), bf16),
                 jax.ShapeDtypeStruct((B, S, LANES), f32)]
    out_specs = [pl.BlockSpec((1, n, tm), lambda b, i: (b, 0, i)),
                 pl.BlockSpec((1, tm, n), lambda b, i: (b, i, 0)),
                 pl.BlockSpec((1, tm // TK_SOFTMAX, n, TK_SOFTMAX), lambda b, i: (b, i, 0, 0)),
                 pl.BlockSpec((1, tm, LANES), lambda b, i: (b, i, 0))]
    return pl.pallas_call(
        _proj_c_kernel, grid=(B, S // tm), in_specs=in_specs, out_specs=out_specs, out_shape=out_shape,
        compiler_params=_params("parallel", "parallel"), name="proj_c",
    )(x, *consts)


CUM_BLOCK = 256


def _split3(v):
    hi = v.astype(bf16)
    r = v - hi.astype(f32)
    mid = r.astype(bf16)
    lo = (r - mid.astype(f32)).astype(bf16)
    return hi, mid, lo


def _fox_bias_kernel(lf_ref, selq_ref, selk_ref, qaug_ref, kaug_ref, carry_ref):
    n = CUM_BLOCK

    @pl.when(pl.program_id(1) == 0)
    def _():
        carry_ref[...] = jnp.zeros(carry_ref.shape, f32)

    lower = (lax.broadcasted_iota(jnp.int32, (n, n), 0) >= lax.broadcasted_iota(jnp.int32, (n, n), 1)).astype(bf16)
    lane = lax.broadcasted_iota(jnp.int32, (n, LANES), 1)
    hi, mid, lo = _split3(lf_ref[0])
    d = _dot(lower, hi) + _dot(lower, mid) + _dot(lower, lo) + carry_ref[...]
    carry_ref[...] = d[n - 1:n, :]
    pieces = [jnp.where(lane == LANES - 1, 1.0, v.astype(f32)).astype(bf16) for v in _split3(d * LOG2E)]
    qaug = _dot_nt(selq_ref[0], pieces[0]) + _dot_nt(selq_ref[1], pieces[1]) + _dot_nt(selq_ref[2], pieces[2])
    kaug = _dot(pieces[0], selk_ref[0]) + _dot(pieces[1], selk_ref[1]) + _dot(pieces[2], selk_ref[2])
    qaug_ref[0] = qaug.astype(bf16)
    kaug_ref[0] = kaug.astype(bf16)


def _fox_selectors():
    selq = np.zeros((3, FOX_HEADS * AUG_ROWS, LANES), np.float32)
    selk = np.zeros((3, LANES, (FOX_HEADS // 2) * LANES), np.float32)
    one = LANES - 1
    for h in range(FOX_HEADS):
        pair, e = divmod(h, 2)
        for c in range(3):
            selq[c, h * AUG_ROWS + 6 * e + c, h] = 1.0
            selk[c, h, pair * LANES + 6 * e + 3 + c] = -1.0
        for c in range(3):
            selq[0, h * AUG_ROWS + 6 * e + 3 + c, one] = 1.0
            selk[0, one, pair * LANES + 6 * e + c] = 1.0
    return jnp.asarray(selq, bf16), jnp.asarray(selk, bf16)


def _fox_bias(lf):
    B, S, _ = lf.shape
    selq, selk = _fox_selectors()
    n = CUM_BLOCK
    nk = (FOX_HEADS // 2) * LANES
    return pl.pallas_call(
        _fox_bias_kernel, grid=(B, S // n),
        in_specs=[pl.BlockSpec((1, n, LANES), lambda b, j: (b, j, 0)), _const_spec(selq.shape), _const_spec(selk.shape)],
        out_specs=[pl.BlockSpec((1, FOX_HEADS * AUG_ROWS, n), lambda b, j: (b, 0, j)),
                   pl.BlockSpec((1, n, nk), lambda b, j: (b, j, 0))],
        out_shape=[jax.ShapeDtypeStruct((B, FOX_HEADS * AUG_ROWS, S), bf16),
                   jax.ShapeDtypeStruct((B, S, nk), bf16)],
        scratch_shapes=[pltpu.VMEM((1, LANES), f32)],
        compiler_params=_params("parallel", "arbitrary"), name="fox_bias",
    )(lf, selq, selk)


def _rotate_half_cols(w):
    half = w.shape[-1] // 2
    return jnp.concatenate([-w[..., half:], w[..., :half]], axis=-1)


def _pad_lanes(w):
    pad = [(0, 0)] * (w.ndim - 1) + [(0, LANES - w.shape[-1])]
    return jnp.pad(w, pad)


def _prep_layer_a(w_in, q_norm, w_uq, kv_norm, w_ukv):
    in_a = Q_LORA + KV_LORA + MLA_ROPE
    n = SB_HEADS * HEAD_DIM
    kr = w_in[:, Q_LORA + KV_LORA:in_a]
    sb = w_in[:, in_a:]
    uq = w_uq.reshape(Q_LORA, MLA_HEADS, MLA_NOPE + MLA_ROPE)
    rope = uq[:, :, MLA_NOPE:]
    ukv = w_ukv.reshape(KV_LORA, MLA_HEADS, MLA_NOPE + HEAD_DIM)
    return {
        "lat": w_in[:, :Q_LORA + KV_LORA].astype(bf16),
        "kr": jnp.concatenate([_pad_lanes(kr), _pad_lanes(_rotate_half_cols(kr))], axis=1).astype(bf16),
        "sbq_t": sb[:, :n].T.astype(bf16),
        "sbk": sb[:, n:2 * n].astype(bf16),
        "sbv_t": sb[:, 2 * n:].T.astype(bf16),
        "gq": q_norm.reshape(1, Q_LORA),
        "gkv": kv_norm.reshape(1, KV_LORA),
        "qn_t": uq[:, :, :MLA_NOPE].reshape(Q_LORA, -1).T.astype(bf16),
        "qra_t": rope.reshape(Q_LORA, -1).T.astype(bf16),
        "qrb_t": _rotate_half_cols(rope).reshape(Q_LORA, -1).T.astype(bf16),
        "kn": ukv[:, :, :MLA_NOPE].reshape(KV_LORA, -1).astype(bf16),
        "v_t": ukv[:, :, MLA_NOPE:].reshape(KV_LORA, -1).T.astype(bf16),
    }


def _prep_layer_c(w_in, b_f):
    n = FOX_HEADS * HEAD_DIM
    return {
        "q_t": w_in[:, :n].T.astype(bf16),
        "k": w_in[:, n:2 * n].astype(bf16),
        "v_t": w_in[:, 2 * n:3 * n].T.astype(bf16),
        "f": _pad_lanes(w_in[:, 3 * n:]).astype(bf16),
        "bf": _pad_lanes(b_f.reshape(1, FOX_HEADS)),
    }


def _rope_tables(seq):
    inv = 1.0 / (ROPE_BASE ** (jnp.arange(0, MLA_ROPE, 2, dtype=f32) / MLA_ROPE))
    ang = jnp.arange(seq, dtype=f32)[:, None] * inv[None, :]
    cos = _pad_lanes(jnp.concatenate([jnp.cos(ang), jnp.cos(ang)], axis=1))
    sin = _pad_lanes(jnp.concatenate([jnp.sin(ang), jnp.sin(ang)], axis=1))
    return cos, sin, cos[:, :MLA_ROPE].T, sin[:, :MLA_ROPE].T


def kernel(x, p, a_w_in, a_q_norm, a_w_uq, a_kv_norm, a_w_ukv, a_w_out, c_w_in, c_b_f, c_w_out,
           ffn_w1, ffn_w3, ffn_w2, ln1_g, ln1_b, ln2_g, ln2_b, ple_w_proj, ple_w_gate, ple_b_gate):
    B, S, D = x.shape
    M = B * S
    row = lambda v: v.reshape(1, -1)

    def layer_tail(parts, wo_parts, xin, i):
        consts = [row(ln1_g[i]), row(ln1_b[i]), ffn_w1[i].astype(bf16), ffn_w3[i].astype(bf16),
                  ffn_w2[i].astype(bf16), row(ln2_g[i]), row(ln2_b[i]), ple_w_gate[i].astype(bf16),
                  row(ple_b_gate[i]), ple_w_proj[i].astype(bf16)]
        return _layer_tail(parts, wo_parts, xin, p[i].reshape(M, P_DIM), consts)

    wa = _prep_layer_a(a_w_in[0], a_q_norm[0], a_w_uq[0], a_kv_norm[0], a_w_ukv[0])
    cos, sin, cost, sint = _rope_tables(S)
    qnt, qrt, kn, kr, vat, qbt, kb, vbt = _proj_a(x, wa, cos, sin, cost, sint)
    o_a = _attn_softmax(qnt, qrt, kn, kr, vat, kx_shared=True, chunk_shift=SEQ_CHUNK.bit_length() - 1, name="attn_mla")
    o_b = _attn_sb(qbt, kb, vbt)
    na = MLA_HEADS * HEAD_DIM
    w_out = a_w_out[0].astype(bf16)
    x2 = layer_tail([o_a.reshape(M, -1), o_b.reshape(M, -1)], [w_out[:na], w_out[na:]], x.reshape(M, D), 0)

    wc = _prep_layer_c(c_w_in[0], c_b_f[0])
    qt, k, vt, lf = _proj_c(x2.reshape(B, S, D), wc)
    qaug, kaug = _fox_bias(lf)
    o_c = _attn_softmax(qt, qaug, k, kaug, vt, kx_shared=False, chunk_shift=0, name="attn_fox")
    x4 = layer_tail([o_c.reshape(M, -1)], [c_w_out[0].astype(bf16)], x2, 1)
    return x4.reshape(B, S, D)
```

```python
import functools

import jax
import jax.numpy as jnp
import numpy as np
from jax import lax
from jax.experimental import pallas as pl
from jax.experimental.pallas import tpu as pltpu

D_MODEL = 1024
SEQ_CHUNK = 64
P_DIM = 256
MLA_HEADS = 8
MLA_NOPE = 64
MLA_ROPE = 32
Q_LORA = 256
KV_LORA = 256
ROPE_BASE = 10000.0
SB_HEADS = 8
FOX_HEADS = 16
HEAD_DIM = 64
DEPTH = 2
DEEPNORM_ALPHA = (2.0 * DEPTH) ** 0.25
LOG2E = 1.4426950408889634
MLA_SCALE = (MLA_NOPE + MLA_ROPE) ** -0.5 * LOG2E
HEAD_SCALE = HEAD_DIM ** -0.5 * LOG2E

LANES = 128
BF16_ROWS = 16
VMEM_LIMIT = 56 * 1024 * 1024

TOK_TILE = 512
TQ = 256
PAIRS_PER_STEP = 4
SCORE_LEAD = 1
TK_SOFTMAX = 256
TK_SB = 128
NEG = -1e30
AUG_ROWS = 16

bf16 = jnp.bfloat16
f32 = jnp.float32


def _dot(a, b):
    return jnp.dot(a, b, preferred_element_type=f32)


def _dot_nt(a, b):
    return lax.dot_general(a, b, (((1,), (1,)), ((), ())), preferred_element_type=f32)


def _layer_norm(z, g, b):
    mu = jnp.mean(z, axis=-1, keepdims=True)
    zc = z - mu
    var = jnp.mean(zc * zc, axis=-1, keepdims=True)
    return zc * lax.rsqrt(var + 1e-5) * g + b


def _rms_norm(c, g):
    return c * lax.rsqrt(jnp.mean(c * c, axis=-1, keepdims=True) + 1e-6) * g


def _log_sigmoid(a):
    return jnp.minimum(a, 0.0) - jnp.log(1.0 + jnp.exp(-jnp.abs(a)))


def _const_spec(shape):
    nd = len(shape)
    return pl.BlockSpec(shape, lambda *_: (0,) * nd, pipeline_mode=pl.Buffered(1))


def _params(*sem):
    return pltpu.CompilerParams(dimension_semantics=sem, vmem_limit_bytes=VMEM_LIMIT)


def _proj_a_kernel(x_ref, wlat_ref, wkr_ref, wsbq_ref, wsbk_ref, wsbv_ref, gq_ref, gkv_ref,
                   wqn_ref, wqra_ref, wqrb_ref, wkn_ref, wv_ref, cos_ref, sin_ref, cost_ref, sint_ref,
                   qnt_ref, qrt_ref, kn_ref, kr_ref, vat_ref, qbt_ref, kb_ref, kbs_ref, vbt_ref):
    xb = x_ref[0].astype(bf16)
    lat = _dot(xb, wlat_ref[...])
    cq = _rms_norm(lat[:, :Q_LORA], gq_ref[...]).astype(bf16)
    ckv = _rms_norm(lat[:, Q_LORA:], gkv_ref[...]).astype(bf16)

    qnt_ref[0] = (_dot_nt(wqn_ref[...], cq) * MLA_SCALE).astype(bf16)
    qa = _dot_nt(wqra_ref[...], cq)
    qb = _dot_nt(wqrb_ref[...], cq)
    cost = cost_ref[...]
    sint = sint_ref[...]
    for h in range(MLA_HEADS):
        sl = slice(h * MLA_ROPE, (h + 1) * MLA_ROPE)
        qrt_ref[0, sl, :] = ((qa[sl] * cost + qb[sl] * sint) * MLA_SCALE).astype(bf16)

    kn_ref[0] = _dot(ckv, wkn_ref[...]).astype(bf16)
    kr2 = _dot(xb, wkr_ref[...])
    kr_ref[0] = (kr2[:, :LANES] * cos_ref[...] + kr2[:, LANES:] * sin_ref[...]).astype(bf16)

    vat = _dot_nt(wv_ref[...], ckv).astype(bf16)
    tm = vat.shape[1]
    for c in range(tm // TK_SOFTMAX):
        vat_ref[0, c] = vat[:, c * TK_SOFTMAX:(c + 1) * TK_SOFTMAX]

    qbt_ref[0] = (_dot_nt(wsbq_ref[...], xb) * HEAD_SCALE).astype(bf16)
    kb = _dot(xb, wsbk_ref[...])
    kb_ref[0] = kb.astype(bf16)
    swapped = [pltpu.roll(kb[:, _tile(t)], HEAD_DIM, axis=1) for t in range(kb.shape[1] // LANES)]
    kbs_ref[0] = jnp.concatenate(swapped, axis=1).astype(bf16)
    vbt = _dot_nt(wsbv_ref[...], xb).astype(bf16)
    for c in range(tm // TK_SB):
        vbt_ref[0, c] = vbt[:, c * TK_SB:(c + 1) * TK_SB]


def _proj_a(x, w, cos, sin, cost, sint):
    B, S, D = x.shape
    tm = TOK_TILE
    na = MLA_HEADS * HEAD_DIM
    nb = SB_HEADS * HEAD_DIM
    consts = [w["lat"], w["kr"], w["sbq_t"], w["sbk"], w["sbv_t"], w["gq"], w["gkv"],
              w["qn_t"], w["qra_t"], w["qrb_t"], w["kn"], w["v_t"]]
    in_specs = [pl.BlockSpec((1, tm, D), lambda b, i: (b, i, 0))]
    in_specs += [_const_spec(c.shape) for c in consts]
    in_specs += [pl.BlockSpec((tm, LANES), lambda b, i: (i, 0)),
                 pl.BlockSpec((tm, LANES), lambda b, i: (i, 0)),
                 pl.BlockSpec((MLA_ROPE, tm), lambda b, i: (0, i)),
                 pl.BlockSpec((MLA_ROPE, tm), lambda b, i: (0, i))]
    out_shape = [
        jax.ShapeDtypeStruct((B, na, S), bf16),
        jax.ShapeDtypeStruct((B, MLA_HEADS * MLA_ROPE, S), bf16),
        jax.ShapeDtypeStruct((B, S, na), bf16),
        jax.ShapeDtypeStruct((B, S, LANES), bf16),
        jax.ShapeDtypeStruct((B, S // TK_SOFTMAX, na, TK_SOFTMAX), bf16),
        jax.ShapeDtypeStruct((B, nb, S), bf16),
        jax.ShapeDtypeStruct((B, S, nb), bf16),
        jax.ShapeDtypeStruct((B, S, nb), bf16),
        jax.ShapeDtypeStruct((B, S // TK_SB, nb, TK_SB), bf16),
    ]
    out_specs = [
        pl.BlockSpec((1, na, tm), lambda b, i: (b, 0, i)),
        pl.BlockSpec((1, MLA_HEADS * MLA_ROPE, tm), lambda b, i: (b, 0, i)),
        pl.BlockSpec((1, tm, na), lambda b, i: (b, i, 0)),
        pl.BlockSpec((1, tm, LANES), lambda b, i: (b, i, 0)),
        pl.BlockSpec((1, tm // TK_SOFTMAX, na, TK_SOFTMAX), lambda b, i: (b, i, 0, 0)),
        pl.BlockSpec((1, nb, tm), lambda b, i: (b, 0, i)),
        pl.BlockSpec((1, tm, nb), lambda b, i: (b, i, 0)),
        pl.BlockSpec((1, tm, nb), lambda b, i: (b, i, 0)),
        pl.BlockSpec((1, tm // TK_SB, nb, TK_SB), lambda b, i: (b, i, 0, 0)),
    ]
    return pl.pallas_call(
        _proj_a_kernel, grid=(B, S // tm), in_specs=in_specs, out_specs=out_specs, out_shape=out_shape,
        compiler_params=_params("parallel", "parallel"), name="proj_a",
    )(x, *consts, cos, sin, cost, sint)


def _pair_queries(q1):
    row = lax.broadcasted_iota(jnp.int32, q1.shape, 0)
    zero = jnp.zeros_like(q1)
    return jnp.concatenate([jnp.where(row < HEAD_DIM, q1, zero), jnp.where(row >= HEAD_DIM, q1, zero)], axis=1)


def _positions(shape, key_start, query_start, tq):
    kpos = key_start + lax.broadcasted_iota(jnp.int32, shape, 0)
    col = lax.broadcasted_iota(jnp.int32, shape, 1)
    qpos = query_start + jnp.where(col >= tq, col - tq, col)
    return kpos, qpos


def _tile(c):
    return slice(c * LANES, (c + 1) * LANES)


def _loop_by_two(start, stop, body):
    n = stop - start

    def two(v, carry):
        body(start + 2 * v)
        body(start + 2 * v + 1)
        return carry

    lax.fori_loop(0, n // 2, two, 0)

    @pl.when(n % 2 == 1)
    def _():
        body(stop - 1)


def _chain_scratch(refs, per_chain):
    return [refs[c * per_chain:(c + 1) * per_chain] for c in range(PAIRS_PER_STEP)]


def _attn_softmax_kernel(q1_ref, qx_ref, k1_ref, kx_ref, vt_ref, o_ref, *scratch, tq, tk, chunk_shift, kx_shared):
    i = pl.program_id(2)
    chains = range(PAIRS_PER_STEP)
    xr = qx_ref.shape[1] // (2 * PAIRS_PER_STEP)
    q2t, m, acc0, acc1, sa, sb = zip(*_chain_scratch(scratch, 6))
    acc = (acc0, acc1)
    for c in chains:
        qx = jnp.concatenate([qx_ref[0, pl.ds(2 * c * xr, xr), :], qx_ref[0, pl.ds((2 * c + 1) * xr, xr), :]], axis=1)
        q2t[c][...] = jnp.concatenate([_pair_queries(q1_ref[0, _tile(c), :]), qx], axis=0)
        m[c][...] = jnp.full(m[c].shape, NEG, f32)
        for h in range(2):
            acc[h][c][...] = jnp.zeros(acc[h][c].shape, f32)
    ones = jnp.ones((BF16_ROWS, tk), bf16)

    def scores(j, s_refs, c):
        ks = pl.multiple_of(j * tk, tk)
        kx = kx_ref[0, pl.ds(ks, tk), pl.ds((0 if kx_shared else c) * LANES, xr)]
        kb = jnp.concatenate([k1_ref[0, pl.ds(ks, tk), _tile(c)], kx], axis=1)
        s_refs[c][...] = _dot(kb, q2t[c][...])

    def accumulate(j, s_refs, masked, j_next=None, next_refs=None):
        if masked:
            kpos, qpos = _positions((tk, 2 * tq), j * tk, i * tq, tq)
            ok = (kpos >> chunk_shift) <= (qpos >> chunk_shift)
        if next_refs is not None:
            for c in range(SCORE_LEAD):
                scores(j_next, next_refs, c)
        for c in chains:
            s = s_refs[c][...]
            if masked:
                s = jnp.where(ok, s, NEG)
            m_old = m[c][...]
            m_new = jnp.maximum(m_old, jnp.max(s, axis=0, keepdims=True))
            alpha = jnp.exp2(m_old - m_new)
            m[c][...] = m_new
            p = jnp.exp2(s - m_new).astype(bf16)
            for h in range(2):
                vta = jnp.concatenate([vt_ref[0, j, pl.ds(c * LANES + h * HEAD_DIM, HEAD_DIM), :], ones], axis=0)
                cols = slice(h * tq, (h + 1) * tq)
                acc[h][c][...] = alpha[:, cols] * acc[h][c][...] + _dot(vta, p[:, cols])
            if next_refs is not None and c + SCORE_LEAD < PAIRS_PER_STEP:
                scores(j_next, next_refs, c + SCORE_LEAD)

    for c in chains:
        scores(0, sa, c)

    def body(u):
        accumulate(2 * u, sa, False, 2 * u + 1, sb)
        accumulate(2 * u + 1, sb, False, 2 * u + 2, sa)

    _loop_by_two(0, i // 2, body)

    @pl.when(i % 2 == 0)
    def _():
        accumulate(i, sa, True)

    @pl.when(i % 2 == 1)
    def _():
        accumulate(i - 1, sa, False, i, sb)
        accumulate(i, sb, True)

    for c in chains:
        heads = []
        for h in range(2):
            a = acc[h][c][...]
            heads.append(a[:HEAD_DIM] * (1.0 / a[HEAD_DIM:HEAD_DIM + 1]))
        o_ref[0, :, _tile(c)] = jnp.concatenate(heads, axis=0).T.astype(o_ref.dtype)


def _attn_softmax(q1t, qxt, k1, kx, vt, *, kx_shared, chunk_shift, name):
    B, n1, S = q1t.shape
    C = PAIRS_PER_STEP
    groups = n1 // (C * LANES)
    tq, tk = TQ, TK_SOFTMAX
    assert tq == tk
    xr = qxt.shape[1] * LANES // (2 * n1)
    if kx_shared:
        kx_spec = pl.BlockSpec((1, S, LANES), lambda b, g, i: (b, 0, 0))
    else:
        kx_spec = pl.BlockSpec((1, S, C * LANES), lambda b, g, i: (b, 0, g))
    kern = functools.partial(_attn_softmax_kernel, tq=tq, tk=tk, chunk_shift=chunk_shift, kx_shared=kx_shared)
    per_chain = [pltpu.VMEM((LANES + xr, 2 * tq), bf16), pltpu.VMEM((1, 2 * tq), f32),
                 pltpu.VMEM((HEAD_DIM + BF16_ROWS, tq), f32), pltpu.VMEM((HEAD_DIM + BF16_ROWS, tq), f32),
                 pltpu.VMEM((tk, 2 * tq), f32), pltpu.VMEM((tk, 2 * tq), f32)]
    return pl.pallas_call(
        kern, grid=(B, groups, S // tq),
        in_specs=[pl.BlockSpec((1, C * LANES, tq), lambda b, g, i: (b, g, i)),
                  pl.BlockSpec((1, 2 * C * xr, tq), lambda b, g, i: (b, g, i)),
                  pl.BlockSpec((1, S, C * LANES), lambda b, g, i: (b, 0, g)),
                  kx_spec,
                  pl.BlockSpec((1, S // tk, C * LANES, tk), lambda b, g, i: (b, 0, g, 0))],
        out_specs=pl.BlockSpec((1, tq, C * LANES), lambda b, g, i: (b, i, g)),
        out_shape=jax.ShapeDtypeStruct((B, S, n1), bf16),
        scratch_shapes=C * per_chain,
        compiler_params=_params("parallel", "parallel", "arbitrary"), name=name,
    )(q1t, qxt, k1, kx, vt)


def _attn_sb_kernel(qt_ref, k_ref, ks_ref, vt_ref, o_ref, *scratch, tq, tk):
    i = pl.program_id(2)
    chains = range(PAIRS_PER_STEP)
    assert tq == 2 * tk
    last = 2 * i + 1
    r, acc0, acc1, z0, z1, lb0, lr0, f0, lb1, lr1, f1 = zip(*_chain_scratch(scratch, 11))
    acc = (acc0, acc1)
    zbuf = (z0, z1)
    sbuf = ((lb0, lr0, f0), (lb1, lr1, f1))
    for c in chains:
        r[c][...] = jnp.zeros(r[c].shape, f32)
        for h in range(2):
            acc[h][c][...] = jnp.zeros(acc[h][c].shape, f32)
    tri = (lax.broadcasted_iota(jnp.int32, (tk, tk), 0) < lax.broadcasted_iota(jnp.int32, (tk, tk), 1)).astype(bf16)

    def half(weights=None, logits=None, stats=None, masked=False):
        if weights is not None:
            t_w, par = weights
            lb, lr, first = sbuf[par]
            sums = [_dot(tri, lr[c][...]) for c in chains]
        if logits is not None:
            t_z, par = logits
            ks = pl.multiple_of((last - t_z) * tk, tk)
            for c in chains:
                for h, keys in enumerate((k_ref, ks_ref)):
                    kh = keys[0, pl.ds(ks, tk), pl.ds(c * LANES, HEAD_DIM)]
                    qh = qt_ref[0, pl.ds(c * LANES + h * HEAD_DIM, HEAD_DIM), :]
                    zbuf[par][c][:, h * tq:(h + 1) * tq] = _dot(kh, qh)
        if stats is not None:
            t_s, par = stats
            lb_s, lr_s, first_s = sbuf[par]
            if masked:
                kpos, qpos = _positions((tk, 2 * tq), (last - t_s) * tk, i * tq, tq)
                past = kpos < qpos
            for c in chains:
                z = zbuf[par][c][...]
                log_beta = jnp.minimum(z, 0.0) - jnp.log2(1.0 + jnp.exp2(-jnp.abs(z)))
                log_rem = log_beta - z
                if masked:
                    log_rem = jnp.where(past, log_rem, 0.0)
                    log_beta = jnp.where(past, log_beta, NEG)
                lr_s[c][...] = log_rem.astype(bf16)
                first_s[c][...] = log_rem[0:1, :]
                lb_s[c][...] = log_beta
        if weights is not None:
            att = []
            for c in chains:
                between = sums[c] + r[c][...]
                r[c][...] = between[0:1, :] + first[c][...]
                att.append(jnp.exp2(lb[c][...] + between).astype(bf16))
            for c in chains:
                for h in range(2):
                    cols = slice(h * tq, (h + 1) * tq)
                    vt = vt_ref[0, last - t_w, pl.ds(c * LANES + h * HEAD_DIM, HEAD_DIM), :]
                    acc[h][c][...] += _dot(vt, att[c][:, cols])

    half(logits=(0, 0))
    half(logits=(1, 1), stats=(0, 0), masked=True)
    half(weights=(0, 0), logits=(jnp.minimum(2, last), 0), stats=(1, 1), masked=True)

    def body(u, carry):
        half(weights=(2 * u - 1, 1), logits=(2 * u + 1, 1), stats=(2 * u, 0))
        half(weights=(2 * u, 0), logits=(2 * u + 2, 0), stats=(2 * u + 1, 1))
        return carry

    lax.fori_loop(1, i, body, 0)

    @pl.when(i >= 1)
    def _():
        half(weights=(2 * i - 1, 1), logits=(last, 1), stats=(2 * i, 0))
        half(weights=(2 * i, 0), stats=(last, 1))

    half(weights=(last, 1))
    for c in chains:
        o_ref[0, :, _tile(c)] = jnp.concatenate([acc0[c][...], acc1[c][...]], axis=0).T.astype(o_ref.dtype)


def _attn_sb(qt, k, ks, vt):
    B, n1, S = qt.shape
    C = PAIRS_PER_STEP
    groups = n1 // (C * LANES)
    tq, tk = TQ, TK_SB
    kern = functools.partial(_attn_sb_kernel, tq=tq, tk=tk)
    return pl.pallas_call(
        kern, grid=(B, groups, S // tq),
        in_specs=[pl.BlockSpec((1, C * LANES, tq), lambda b, g, i: (b, g, i)),
                  pl.BlockSpec((1, S, C * LANES), lambda b, g, i: (b, 0, g)),
                  pl.BlockSpec((1, S, C * LANES), lambda b, g, i: (b, 0, g)),
                  pl.BlockSpec((1, S // tk, C * LANES, tk), lambda b, g, i: (b, 0, g, 0))],
        out_specs=pl.BlockSpec((1, tq, C * LANES), lambda b, g, i: (b, i, g)),
        out_shape=jax.ShapeDtypeStruct((B, S, n1), bf16),
        scratch_shapes=C * (
            [pltpu.VMEM((1, 2 * tq), f32),
             pltpu.VMEM((HEAD_DIM, tq), f32), pltpu.VMEM((HEAD_DIM, tq), f32)]
            + 2 * [pltpu.VMEM((tk, 2 * tq), f32)]
            + 2 * [pltpu.VMEM((tk, 2 * tq), f32), pltpu.VMEM((tk, 2 * tq), bf16), pltpu.VMEM((1, 2 * tq), f32)]),
        compiler_params=_params("parallel", "parallel", "arbitrary"), name="attn_sb",
    )(qt, k, ks, vt)


def _layer_tail_kernel(*refs, n_parts):
    o_refs = refs[:n_parts]
    wo_refs = refs[n_parts:2 * n_parts]
    (x_ref, p_ref, g1_ref, b1_ref, w1_ref, w3_ref, w2_ref, g2_ref, b2_ref,
     wg_ref, bg_ref, wp_ref, y_ref) = refs[2 * n_parts:]
    mix = _dot(o_refs[0][...], wo_refs[0][...])
    for o_ref, w_ref in zip(o_refs[1:], wo_refs[1:]):
        mix = mix + _dot(o_ref[...], w_ref[...])
    x1 = _layer_norm(DEEPNORM_ALPHA * x_ref[...] + mix, g1_ref[...], b1_ref[...])
    xb = x1.astype(bf16)
    h1 = _dot(xb, w1_ref[...])
    h3 = _dot(xb, w3_ref[...])
    act = (h1 * jax.nn.sigmoid(h1) * h3).astype(bf16)
    x2 = _layer_norm(DEEPNORM_ALPHA * x1 + _dot(act, w2_ref[...]), g2_ref[...], b2_ref[...])
    gate = jax.nn.sigmoid(_dot(x2.astype(bf16), wg_ref[...]) + bg_ref[...])
    y_ref[...] = x2 + gate * _dot(p_ref[...].astype(bf16), wp_ref[...])


def _layer_tail(parts, wo_parts, x, p, consts):
    M, D = x.shape
    tm = TOK_TILE
    in_specs = [pl.BlockSpec((tm, o.shape[1]), lambda i: (i, 0)) for o in parts]
    in_specs += [_const_spec(w.shape) for w in wo_parts]
    in_specs += [pl.BlockSpec((tm, D), lambda i: (i, 0)), pl.BlockSpec((tm, p.shape[1]), lambda i: (i, 0))]
    in_specs += [_const_spec(c.shape) for c in consts]
    return pl.pallas_call(
        functools.partial(_layer_tail_kernel, n_parts=len(parts)), grid=(M // tm,), in_specs=in_specs,
        out_specs=pl.BlockSpec((tm, D), lambda i: (i, 0)),
        out_shape=jax.ShapeDtypeStruct((M, D), f32),
        compiler_params=_params("parallel"), name="layer_tail",
    )(*parts, *wo_parts, x, p, *consts)


def _proj_c_kernel(x_ref, wq_ref, wk_ref, wv_ref, wf_ref, bf_ref, qt_ref, k_ref, vt_ref, lf_ref):
    xb = x_ref[0].astype(bf16)
    qt_ref[0] = (_dot_nt(wq_ref[...], xb) * HEAD_SCALE).astype(bf16)
    k_ref[0] = _dot(xb, wk_ref[...]).astype(bf16)
    vt = _dot_nt(wv_ref[...], xb).astype(bf16)
    tm = vt.shape[1]
    for c in range(tm // TK_SOFTMAX):
        vt_ref[0, c] = vt[:, c * TK_SOFTMAX:(c + 1) * TK_SOFTMAX]
    lf_ref[0] = _log_sigmoid(_dot(xb, wf_ref[...]) + bf_ref[...])


def _proj_c(x, w):
    B, S, D = x.shape
    tm = TOK_TILE
    n = FOX_HEADS * HEAD_DIM
    consts = [w["q_t"], w["k"], w["v_t"], w["f"], w["bf"]]
    in_specs = [pl.BlockSpec((1, tm, D), lambda b, i: (b, i, 0))] + [_const_spec(c.shape) for c in consts]
    out_shape = [jax.ShapeDtypeStruct((B, n, S), bf16),
                 jax.ShapeDtypeStruct((B, S, n), bf16),
                 jax.ShapeDtypeStruct((B, S // TK_SOFTMAX, n, TK_SOFTMAX), bf16),
                 jax.ShapeDtypeStruct((B, S, LANES), f32)]
    out_specs = [pl.BlockSpec((1, n, tm), lambda b, i: (b, 0, i)),
                 pl.BlockSpec((1, tm, n), lambda b, i: (b, i, 0)),
                 pl.BlockSpec((1, tm // TK_SOFTMAX, n, TK_SOFTMAX), lambda b, i: (b, i, 0, 0)),
                 pl.BlockSpec((1, tm, LANES), lambda b, i: (b, i, 0))]
    return pl.pallas_call(
        _proj_c_kernel, grid=(B, S // tm), in_specs=in_specs, out_specs=out_specs, out_shape=out_shape,
        compiler_params=_params("parallel", "parallel"), name="proj_c",
    )(x, *consts)


CUM_BLOCK = 256


def _split3(v):
    hi = v.astype(bf16)
    r = v - hi.astype(f32)
    mid = r.astype(bf16)
    lo = (r - mid.astype(f32)).astype(bf16)
    return hi, mid, lo


def _fox_bias_kernel(lf_ref, selq_ref, selk_ref, qaug_ref, kaug_ref, carry_ref):
    n = CUM_BLOCK

    @pl.when(pl.program_id(1) == 0)
    def _():
        carry_ref[...] = jnp.zeros(carry_ref.shape, f32)

    lower = (lax.broadcasted_iota(jnp.int32, (n, n), 0) >= lax.broadcasted_iota(jnp.int32, (n, n), 1)).astype(bf16)
    lane = lax.broadcasted_iota(jnp.int32, (n, LANES), 1)
    hi, mid, lo = _split3(lf_ref[0])
    d = _dot(lower, hi) + _dot(lower, mid) + _dot(lower, lo) + carry_ref[...]
    carry_ref[...] = d[n - 1:n, :]
    pieces = [jnp.where(lane == LANES - 1, 1.0, v.astype(f32)).astype(bf16) for v in _split3(d * LOG2E)]
    qaug = _dot_nt(selq_ref[0], pieces[0]) + _dot_nt(selq_ref[1], pieces[1]) + _dot_nt(selq_ref[2], pieces[2])
    kaug = _dot(pieces[0], selk_ref[0]) + _dot(pieces[1], selk_ref[1]) + _dot(pieces[2], selk_ref[2])
    qaug_ref[0] = qaug.astype(bf16)
    kaug_ref[0] = kaug.astype(bf16)


def _fox_selectors():
    selq = np.zeros((3, FOX_HEADS * AUG_ROWS, LANES), np.float32)
    selk = np.zeros((3, LANES, (FOX_HEADS // 2) * LANES), np.float32)
    one = LANES - 1
    for h in range(FOX_HEADS):
        pair, e = divmod(h, 2)
        for c in range(3):
            selq[c, h * AUG_ROWS + 6 * e + c, h] = 1.0
            selk[c, h, pair * LANES + 6 * e + 3 + c] = -1.0
        for c in range(3):
            selq[0, h * AUG_ROWS + 6 * e + 3 + c, one] = 1.0
            selk[0, one, pair * LANES + 6 * e + c] = 1.0
    return jnp.asarray(selq, bf16), jnp.asarray(selk, bf16)


def _fox_bias(lf):
    B, S, _ = lf.shape
    selq, selk = _fox_selectors()
    n = CUM_BLOCK
    nk = (FOX_HEADS // 2) * LANES
    return pl.pallas_call(
        _fox_bias_kernel, grid=(B, S // n),
        in_specs=[pl.BlockSpec((1, n, LANES), lambda b, j: (b, j, 0)), _const_spec(selq.shape), _const_spec(selk.shape)],
        out_specs=[pl.BlockSpec((1, FOX_HEADS * AUG_ROWS, n), lambda b, j: (b, 0, j)),
                   pl.BlockSpec((1, n, nk), lambda b, j: (b, j, 0))],
        out_shape=[jax.ShapeDtypeStruct((B, FOX_HEADS * AUG_ROWS, S), bf16),
                   jax.ShapeDtypeStruct((B, S, nk), bf16)],
        scratch_shapes=[pltpu.VMEM((1, LANES), f32)],
        compiler_params=_params("parallel", "arbitrary"), name="fox_bias",
    )(lf, selq, selk)


def _rotate_half_cols(w):
    half = w.shape[-1] // 2
    return jnp.concatenate([-w[..., half:], w[..., :half]], axis=-1)


def _pad_lanes(w):
    pad = [(0, 0)] * (w.ndim - 1) + [(0, LANES - w.shape[-1])]
    return jnp.pad(w, pad)


def _prep_layer_a(w_in, q_norm, w_uq, kv_norm, w_ukv):
    in_a = Q_LORA + KV_LORA + MLA_ROPE
    n = SB_HEADS * HEAD_DIM
    kr = w_in[:, Q_LORA + KV_LORA:in_a]
    sb = w_in[:, in_a:]
    uq = w_uq.reshape(Q_LORA, MLA_HEADS, MLA_NOPE + MLA_ROPE)
    rope = uq[:, :, MLA_NOPE:]
    ukv = w_ukv.reshape(KV_LORA, MLA_HEADS, MLA_NOPE + HEAD_DIM)
    return {
        "lat": w_in[:, :Q_LORA + KV_LORA].astype(bf16),
        "kr": jnp.concatenate([_pad_lanes(kr), _pad_lanes(_rotate_half_cols(kr))], axis=1).astype(bf16),
        "sbq_t": sb[:, :n].T.astype(bf16),
        "sbk": sb[:, n:2 * n].astype(bf16),
        "sbv_t": sb[:, 2 * n:].T.astype(bf16),
        "gq": q_norm.reshape(1, Q_LORA),
        "gkv": kv_norm.reshape(1, KV_LORA),
        "qn_t": uq[:, :, :MLA_NOPE].reshape(Q_LORA, -1).T.astype(bf16),
        "qra_t": rope.reshape(Q_LORA, -1).T.astype(bf16),
        "qrb_t": _rotate_half_cols(rope).reshape(Q_LORA, -1).T.astype(bf16),
        "kn": ukv[:, :, :MLA_NOPE].reshape(KV_LORA, -1).astype(bf16),
        "v_t": ukv[:, :, MLA_NOPE:].reshape(KV_LORA, -1).T.astype(bf16),
    }


def _prep_layer_c(w_in, b_f):
    n = FOX_HEADS * HEAD_DIM
    return {
        "q_t": w_in[:, :n].T.astype(bf16),
        "k": w_in[:, n:2 * n].astype(bf16),
        "v_t": w_in[:, 2 * n:3 * n].T.astype(bf16),
        "f": _pad_lanes(w_in[:, 3 * n:]).astype(bf16),
        "bf": _pad_lanes(b_f.reshape(1, FOX_HEADS)),
    }


def _rope_tables(seq):
    inv = 1.0 / (ROPE_BASE ** (jnp.arange(0, MLA_ROPE, 2, dtype=f32) / MLA_ROPE))
    ang = jnp.arange(seq, dtype=f32)[:, None] * inv[None, :]
    cos = _pad_lanes(jnp.concatenate([jnp.cos(ang), jnp.cos(ang)], axis=1))
    sin = _pad_lanes(jnp.concatenate([jnp.sin(ang), jnp.sin(ang)], axis=1))
    return cos, sin, cos[:, :MLA_ROPE].T, sin[:, :MLA_ROPE].T


def kernel(x, p, a_w_in, a_q_norm, a_w_uq, a_kv_norm, a_w_ukv, a_w_out, c_w_in, c_b_f, c_w_out,
           ffn_w1, ffn_w3, ffn_w2, ln1_g, ln1_b, ln2_g, ln2_b, ple_w_proj, ple_w_gate, ple_b_gate):
    B, S, D = x.shape
    M = B * S
    row = lambda v: v.reshape(1, -1)

    def layer_tail(parts, wo_parts, xin, i):
        consts = [row(ln1_g[i]), row(ln1_b[i]), ffn_w1[i].astype(bf16), ffn_w3[i].astype(bf16),
                  ffn_w2[i].astype(bf16), row(ln2_g[i]), row(ln2_b[i]), ple_w_gate[i].astype(bf16),
                  row(ple_b_gate[i]), ple_w_proj[i].astype(bf16)]
        return _layer_tail(parts, wo_parts, xin, p[i].reshape(M, P_DIM), consts)

    wa = _prep_layer_a(a_w_in[0], a_q_norm[0], a_w_uq[0], a_kv_norm[0], a_w_ukv[0])
    cos, sin, cost, sint = _rope_tables(S)
    qnt, qrt, kn, kr, vat, qbt, kb, kbs, vbt = _proj_a(x, wa, cos, sin, cost, sint)
    o_a = _attn_softmax(qnt, qrt, kn, kr, vat, kx_shared=True, chunk_shift=SEQ_CHUNK.bit_length() - 1, name="attn_mla")
    o_b = _attn_sb(qbt, kb, kbs, vbt)
    na = MLA_HEADS * HEAD_DIM
    w_out = a_w_out[0].astype(bf16)
    x2 = layer_tail([o_a.reshape(M, -1), o_b.reshape(M, -1)], [w_out[:na], w_out[na:]], x.reshape(M, D), 0)

    wc = _prep_layer_c(c_w_in[0], c_b_f[0])
    qt, k, vt, lf = _proj_c(x2.reshape(B, S, D), wc)
    qaug, kaug = _fox_bias(lf)
    o_c = _attn_softmax(qt, qaug, k, kaug, vt, kx_shared=False, chunk_shift=0, name="attn_fox")
    x4 = layer_tail([o_c.reshape(M, -1)], [c_w_out[0].astype(bf16)], x2, 1)
    return x4.reshape(B, S, D)
```
